```python
import math
import jax
import jax.numpy as jnp
from jax import lax
import numpy as np

D_MODEL = 1024
BATCH = 32
SEQ = 2048
DEPTH = 1

CHUNK = 64
MIX_WIDTH = D_MODEL
ATT_WIDTH = MIX_WIDTH // 2
N_ATT_HEADS = 8
ATT_HEAD_DIM = ATT_WIDTH // N_ATT_HEADS
KV_LATENT = 128
N_IDX_HEADS = 8
IDX_HEAD_DIM = 64
TOPK_MAX = 256
Q_BLOCK = 128
POOL_WIDTH = MIX_WIDTH - ATT_WIDTH
POOL_WINDOWS = (2, 4, 8, 16)
N_POOL_GROUPS = len(POOL_WINDOWS)
POOL_GROUP = POOL_WIDTH // N_POOL_GROUPS
D_FF = 2816
CONV_WIDTH = 3
EPS = 1e-6

COL_Q = N_ATT_HEADS * ATT_HEAD_DIM
COL_C = KV_LATENT
COL_QI = N_IDX_HEADS * IDX_HEAD_DIM
COL_KI = IDX_HEAD_DIM
COL_WI = N_IDX_HEADS
COL_U = POOL_WIDTH
SPLITS = tuple(np.cumsum([COL_Q, COL_C, COL_QI, COL_KI, COL_WI]).tolist())
IN_COLS = COL_Q + COL_C + COL_QI + COL_KI + COL_WI + COL_U

kernel_name = "hybrid_dsa_pool_convffn"


def rmsnorm(x, g):
    xf = x.astype(jnp.float32)
    y = xf * lax.rsqrt(jnp.mean(xf * xf, axis=-1, keepdims=True) + EPS)
    return (y * g.astype(jnp.float32)).astype(x.dtype)


def alibi_slopes(n_heads):
    h = jnp.arange(1, n_heads + 1, dtype=jnp.float32)
    return jnp.exp2(-8.0 * h / n_heads)


def dsa_attention(q, c, q_idx, k_idx, w_idx, w_uk, w_uv):
    B, S, H, dh = q.shape
    n_top = min(TOPK_MAX, S // 4)
    nb = S // Q_BLOCK
    scale = ATT_HEAD_DIM ** -0.5
    w_idx = w_idx * (N_IDX_HEADS ** -0.5) * (IDX_HEAD_DIM ** -0.5)
    q_lat = jnp.einsum("bshd,hdc->bshc", q, w_uk)
    slopes = alibi_slopes(H)
    key_chunk = jnp.arange(S, dtype=jnp.int32) // CHUNK
    qpos_blocks = jnp.arange(S, dtype=jnp.int32).reshape(nb, Q_BLOCK)

    def to_blocks(a):
        return a.reshape((B, nb, Q_BLOCK) + a.shape[2:]).swapaxes(0, 1)

    def block(args):
        ql, qi, wi, qpos = args
        rel = jax.nn.relu(jnp.einsum("bqid,bsd->bqis", qi, k_idx).astype(jnp.float32))
        score = jnp.einsum("bqi,bqis->bqs", wi.astype(jnp.float32), rel)
        admissible = key_chunk[None, :] <= (qpos // CHUNK)[:, None]
        score = jnp.where(admissible[None], score, -jnp.inf)
        vals, idx = lax.top_k(score, n_top)
        valid = jnp.isfinite(vals)
        c_sel = jax.vmap(lambda cb, ib: cb[ib])(c, idx)
        logits = jnp.einsum("bqhc,bqkc->bqhk", ql, c_sel).astype(jnp.float32) * scale
        dist = jnp.abs(qpos[None, :, None] - idx).astype(jnp.float32)
        logits = logits - slopes[None, None, :, None] * dist[:, :, None, :]
        logits = jnp.where(valid[:, :, None, :], logits, -jnp.inf)
        p = jax.nn.softmax(logits, axis=-1).astype(c.dtype)
        return jnp.einsum("bqhk,bqkc->bqhc", p, c_sel)

    o_lat = lax.map(block, (to_blocks(q_lat), to_blocks(q_idx), to_blocks(w_idx), qpos_blocks))
    o_lat = o_lat.swapaxes(0, 1).reshape(B, S, H, KV_LATENT)
    o = jnp.einsum("bshc,hcd->bshd", o_lat, w_uv)
    return o.reshape(B, S, H * dh)


def multiscale_pool(u, w_pool, pool_scale):
    B, S, W = u.shape
    uf = u.astype(jnp.float32)
    cs = jnp.concatenate([jnp.zeros((B, 1, W), jnp.float32), jnp.cumsum(uf, axis=1)], axis=1)
    t = jnp.arange(S, dtype=jnp.int32)
    outs = []
    for g, win in enumerate(POOL_WINDOWS):
        sl = slice(g * POOL_GROUP, (g + 1) * POOL_GROUP)
        start = jnp.maximum(t + 1 - win, 0)
        win_sum = cs[:, t + 1, sl] - cs[:, start, sl]
        count = (t + 1 - start).astype(jnp.float32)
        outs.append(win_sum / count[None, :, None] - uf[:, :, sl])
    pooled = jnp.stack(outs, axis=2)
    mixed = jnp.einsum("bsgc,gcd->bsgd", pooled, w_pool.astype(jnp.float32)).reshape(B, S, W)
    return (mixed * pool_scale.astype(jnp.float32)).astype(u.dtype)


def conv_glu_ffn(h, w_up, conv_w, conv_b, w_down):
    S = h.shape[1]
    a = h @ w_up
    ap = jnp.pad(a, ((0, 0), (CONV_WIDTH - 1, 0), (0, 0)))
    conv = conv_b + sum(ap[:, j:j + S] * conv_w[j] for j in range(CONV_WIDTH))
    gate, up = jnp.split(conv, 2, axis=-1)
    return (jax.nn.silu(gate) * up) @ w_down


def setup_inputs(seed: int = 0) -> dict:
    key = jax.random.key(seed)
    ks = jax.random.split(key, 16)
    L = DEPTH
    f32 = jnp.float32
    nrm = lambda k, shape, s: jax.random.normal(k, shape, f32) * s
    return {
        "x": jax.random.normal(ks[0], (BATCH, SEQ, D_MODEL), f32),
        "g_mix": 1.0 + nrm(ks[1], (L, D_MODEL), 0.01),
        "w_in": nrm(ks[2], (L, D_MODEL, IN_COLS), D_MODEL ** -0.5),
        "g_kv": 1.0 + nrm(ks[3], (L, KV_LATENT), 0.01),
        "w_uk": nrm(ks[4], (L, N_ATT_HEADS, ATT_HEAD_DIM, KV_LATENT), ATT_HEAD_DIM ** -0.5),
        "w_uv": nrm(ks[5], (L, N_ATT_HEADS, KV_LATENT, ATT_HEAD_DIM), KV_LATENT ** -0.5),
        "w_pool": nrm(ks[6], (L, N_POOL_GROUPS, POOL_GROUP, POOL_GROUP), POOL_GROUP ** -0.5),
        "pool_scale": 1.0 + nrm(ks[7], (L, POOL_WIDTH), 0.1),
        "w_o": nrm(ks[8], (L, MIX_WIDTH, D_MODEL), MIX_WIDTH ** -0.5),
        "g_ffn": 1.0 + nrm(ks[9], (L, D_MODEL), 0.01),
        "w_up": nrm(ks[10], (L, D_MODEL, 2 * D_FF), D_MODEL ** -0.5),
        "conv_w": nrm(ks[11], (L, CONV_WIDTH, 2 * D_FF), CONV_WIDTH ** -0.5),
        "conv_b": nrm(ks[12], (L, 2 * D_FF), 0.01),
        "w_down": nrm(ks[13], (L, D_FF, D_MODEL), D_FF ** -0.5),
        "g_final": 1.0 + nrm(ks[14], (D_MODEL,), 0.01),
    }


def reference(x, g_mix, w_in, g_kv, w_uk, w_uv, w_pool, pool_scale, w_o,
              g_ffn, w_up, conv_w, conv_b, w_down, g_final):
    B, S, _ = x.shape
    for l in range(DEPTH):
        h = rmsnorm(x, g_mix[l])
        proj = h @ w_in[l]
        q, c_raw, q_idx, k_idx, w_idx, u_pool = jnp.split(proj, SPLITS, axis=-1)
        q = q.reshape(B, S, N_ATT_HEADS, ATT_HEAD_DIM)
        c = rmsnorm(c_raw, g_kv[l])
        q_idx = q_idx.reshape(B, S, N_IDX_HEADS, IDX_HEAD_DIM)
        a_out = dsa_attention(q, c, q_idx, k_idx, w_idx, w_uk[l], w_uv[l])
        b_out = multiscale_pool(u_pool, w_pool[l], pool_scale[l])
        x = x + jnp.concatenate([a_out, b_out], axis=-1) @ w_o[l]
        x = x + conv_glu_ffn(rmsnorm(x, g_ffn[l]), w_up[l], conv_w[l], conv_b[l], w_down[l])
    return rmsnorm(x, g_final)
```

```python
import functools

import jax
import jax.numpy as jnp
from jax import lax
from jax.experimental import pallas as pl
from jax.experimental.pallas import tpu as pltpu

CHUNK = 64
N_ATT_HEADS = 8
ATT_HEAD_DIM = 64
KV_LATENT = 128
N_IDX_HEADS = 8
IDX_HEAD_DIM = 64
TOPK_MAX = 256
POOL_WINDOWS = (2, 4, 8, 16)
POOL_GROUP = 128
CONV_WIDTH = 3
EPS = 1e-6

COL_Q = N_ATT_HEADS * ATT_HEAD_DIM
COL_QI = N_IDX_HEADS * IDX_HEAD_DIM
POOL_WIDTH = POOL_GROUP * len(POOL_WINDOWS)
TAIL = 128
POOL_HALO = 16
CONV_HALO = 8
MASKED_DIST = 1e30
BISECT_STEPS = 8
MAX_BISECT_ROUNDS = 64
VMEM_LIMIT = 56 * 1024 * 1024

F32 = jnp.float32
BF16 = jnp.bfloat16


def _rms(x, g):
    return x * lax.rsqrt(jnp.mean(x * x, axis=-1, keepdims=True) + EPS) * g


def _proj_kernel(x_ref, g_ref, w_ref, wuk_ref, gkv_ref,
                 ql_ref, c_ref, ct_ref, qi_ref, kt_ref, wi_ref, u_ref):
    h = _rms(x_ref[...], g_ref[...]).astype(BF16)
    p = jnp.dot(h, w_ref[...], preferred_element_type=F32)
    o_c = COL_Q
    o_qi = o_c + KV_LATENT
    o_u = o_qi + COL_QI
    o_t = o_u + POOL_WIDTH
    q = p[:, :o_c].astype(BF16)
    for hd in range(N_ATT_HEADS):
        ql = jnp.dot(q[:, hd * ATT_HEAD_DIM:(hd + 1) * ATT_HEAD_DIM], wuk_ref[hd],
                     preferred_element_type=F32)
        ql_ref[hd] = (ql * (ATT_HEAD_DIM ** -0.5)).astype(BF16)
    c = _rms(p[:, o_c:o_qi], gkv_ref[...])
    c_ref[...] = c.astype(BF16)
    ct_ref[...] = c.T.astype(BF16)
    qi_ref[...] = p[:, o_qi:o_u].astype(BF16)
    u_ref[...] = p[:, o_u:o_t]
    tail = p[:, o_t:o_t + TAIL]
    kt_ref[...] = tail.T[:IDX_HEAD_DIM].astype(BF16)
    wi_ref[...] = tail[:, IDX_HEAD_DIM:IDX_HEAD_DIM + N_IDX_HEADS] * (
        (N_IDX_HEADS ** -0.5) * (IDX_HEAD_DIM ** -0.5))


def _proj(x, g, w_cat, wuk, gkv, tm):
    B, S, D = x.shape
    ncol = w_cat.shape[1]
    grid = (B, S // tm)
    const = lambda *shape: pl.BlockSpec(shape, lambda b, i: (0,) * len(shape))
    return pl.pallas_call(
        _proj_kernel,
        grid=grid,
        in_specs=[
            pl.BlockSpec((None, tm, D), lambda b, i: (b, i, 0)),
            const(1, D),
            const(D, ncol),
            const(N_ATT_HEADS, ATT_HEAD_DIM, KV_LATENT),
            const(1, KV_LATENT),
        ],
        out_specs=[
            pl.BlockSpec((None, N_ATT_HEADS, tm, KV_LATENT), lambda b, i: (b, 0, i, 0)),
            pl.BlockSpec((None, tm, KV_LATENT), lambda b, i: (b, i, 0)),
            pl.BlockSpec((None, KV_LATENT, tm), lambda b, i: (b, 0, i)),
            pl.BlockSpec((None, tm, COL_QI), lambda b, i: (b, i, 0)),
            pl.BlockSpec((None, IDX_HEAD_DIM, tm), lambda b, i: (b, 0, i)),
            pl.BlockSpec((None, tm, N_IDX_HEADS), lambda b, i: (b, i, 0)),
            pl.BlockSpec((None, tm, POOL_WIDTH), lambda b, i: (b, i, 0)),
        ],
        out_shape=[
            jax.ShapeDtypeStruct((B, N_ATT_HEADS, S, KV_LATENT), BF16),
            jax.ShapeDtypeStruct((B, S, KV_LATENT), BF16),
            jax.ShapeDtypeStruct((B, KV_LATENT, S), BF16),
            jax.ShapeDtypeStruct((B, S, COL_QI), BF16),
            jax.ShapeDtypeStruct((B, IDX_HEAD_DIM, S), BF16),
            jax.ShapeDtypeStruct((B, S, N_IDX_HEADS), F32),
            jax.ShapeDtypeStruct((B, S, POOL_WIDTH), F32),
        ],
        compiler_params=pltpu.CompilerParams(
            dimension_semantics=("parallel", "parallel"), vmem_limit_bytes=VMEM_LIMIT),
        name="proj",
    )(x, g, w_cat, wuk, gkv)


def _attn_kernel(ql_ref, qi_ref, wi_ref, kt_ref, c_ref, ct_ref, wuv_ref, a_ref,
                 sm_ref, lo_ref, hi_ref, clo_ref, chi_ref, *, n_top):
    qb = qi_ref.shape[0]
    S = kt_ref.shape[1]
    j = pl.program_id(1)

    qi = qi_ref[...]
    kt = kt_ref[...]
    wi = wi_ref[...]
    score = jnp.zeros((qb, S), F32)
    for i in range(N_IDX_HEADS):
        rel = jnp.dot(qi[:, i * IDX_HEAD_DIM:(i + 1) * IDX_HEAD_DIM], kt,
                      preferred_element_type=F32)
        score = score + wi[:, i:i + 1] * jnp.maximum(rel, 0.0)

    qpos = j * qb + lax.broadcasted_iota(jnp.int32, (qb, 1), 0)
    kpos = lax.broadcasted_iota(jnp.int32, (1, S), 1)
    qchunk = qpos // CHUNK
    adm = (kpos // CHUNK) <= qchunk
    sm_ref[...] = jnp.where(adm, score, -jnp.inf)

    n_adm = ((qchunk + 1) * CHUNK).astype(F32)
    k_sel = jnp.minimum(n_adm, float(n_top))
    row_min = jnp.min(jnp.where(adm, score, jnp.inf), axis=1, keepdims=True)
    row_max = jnp.max(sm_ref[...], axis=1, keepdims=True)
    lo_ref[...] = row_min
    hi_ref[...] = row_max + (jnp.abs(row_max) * (2.0 ** -20) + 1e-30)
    clo_ref[...] = n_adm
    chi_ref[...] = jnp.zeros_like(n_adm)

    def count_ge(t):
        return jnp.sum(jnp.where(sm_ref[...] >= t, 1.0, 0.0), axis=1, keepdims=True)

    def unresolved():
        s = sm_ref[...]
        lo = lo_ref[...]
        hi = hi_ref[...]
        v_lo = jnp.min(jnp.where(s >= lo, s, jnp.inf), axis=1, keepdims=True)
        v_hi = jnp.max(jnp.where(s < hi, s, -jnp.inf), axis=1, keepdims=True)
        open_rows = jnp.logical_and(clo_ref[...] != k_sel, v_lo != v_hi)
        return (jnp.max(jnp.where(open_rows, 1.0, 0.0)) > 0.0).astype(jnp.int32)

    def bisect_round(carry):
        rounds, _ = carry
        for _ in range(BISECT_STEPS):
            lo = lo_ref[...]
            hi = hi_ref[...]
            mid = lo + 0.5 * (hi - lo)
            cnt = count_ge(mid)
            ge = cnt >= k_sel
            lo_ref[...] = jnp.where(ge, mid, lo)
            hi_ref[...] = jnp.where(ge, hi, mid)
            clo_ref[...] = jnp.where(ge, cnt, clo_ref[...])
            chi_ref[...] = jnp.where(ge, chi_ref[...], cnt)
        return rounds + 1, unresolved()

    lax.while_loop(lambda c: jnp.logical_and(c[1] > 0, c[0] < MAX_BISECT_ROUNDS),
                   bisect_round, (jnp.int32(0), ((j + 1) * qb > n_top).astype(jnp.int32)))

    s = sm_ref[...]
    lo = lo_ref[...]
    hi = hi_ref[...]
    in_range = jnp.logical_and(s >= lo, s < hi)
    need = k_sel - chi_ref[...]
    kposf = kpos.astype(F32)

    def tie_cut():
        def step(_, bounds):
            lo_i, hi_i = bounds
            mid_i = jnp.floor(0.5 * (lo_i + hi_i))
            cnt = jnp.sum(jnp.where(jnp.logical_and(in_range, kposf <= mid_i), 1.0, 0.0),
                          axis=1, keepdims=True)
            ok = cnt >= need
            return jnp.where(ok, lo_i, mid_i), jnp.where(ok, mid_i, hi_i)
        n_steps = max(1, (S - 1).bit_length()) + 1
        init = (jnp.full((qb, 1), -1.0, F32), jnp.full((qb, 1), float(S - 1), F32))
        return lax.fori_loop(0, n_steps, step, init)[1]

    has_tie = jnp.max(jnp.where(clo_ref[...] != k_sel, 1.0, 0.0)) > 0.0
    cut = lax.cond(has_tie, tie_cut, lambda: jnp.full((qb, 1), float(S - 1), F32))
    sel = jnp.logical_or(s >= hi, jnp.logical_and(in_range, kposf <= cut))

    dist = jnp.abs(qpos - kpos).astype(F32)
    distm = jnp.where(sel, dist, MASKED_DIST)

    c = c_ref[...]
    ct = ct_ref[...]
    o_heads = []
    for hd in range(N_ATT_HEADS):
        slope = 2.0 ** (-8.0 * (hd + 1) / N_ATT_HEADS)
        logits = jnp.dot(ql_ref[hd], ct, preferred_element_type=F32) - slope * distm
        m = jnp.max(logits, axis=1, keepdims=True)
        p = jnp.exp(logits - m)
        l = jnp.sum(p, axis=1, keepdims=True)
        o = jnp.dot(p.astype(BF16), c, preferred_element_type=F32) / l
        o_heads.append(o.astype(BF16))
    o_lat = jnp.concatenate(o_heads, axis=1)
    a_ref[...] = jnp.dot(o_lat, wuv_ref[...], preferred_element_type=F32).astype(BF16)


def _attn(ql, qi, wi, kt, c, ct, wuv_bd, qb, n_top):
    B, H, S, dc = ql.shape
    grid = (B, S // qb)
    small = pltpu.VMEM((qb, 1), F32)
    return pl.pallas_call(
        functools.partial(_attn_kernel, n_top=n_top),
        grid=grid,
        in_specs=[
            pl.BlockSpec((None, H, qb, dc), lambda b, j: (b, 0, j, 0)),
            pl.BlockSpec((None, qb, COL_QI), lambda b, j: (b, j, 0)),
            pl.BlockSpec((None, qb, N_IDX_HEADS), lambda b, j: (b, j, 0)),
            pl.BlockSpec((None, IDX_HEAD_DIM, S), lambda b, j: (b, 0, 0)),
            pl.BlockSpec((None, S, dc), lambda b, j: (b, 0, 0)),
            pl.BlockSpec((None, dc, S), lambda b, j: (b, 0, 0)),
            pl.BlockSpec((H * dc, COL_Q), lambda b, j: (0, 0)),
        ],
        out_specs=pl.BlockSpec((None, qb, COL_Q), lambda b, j: (b, j, 0)),
        out_shape=jax.ShapeDtypeStruct((B, S, COL_Q), BF16),
        scratch_shapes=[pltpu.VMEM((qb, S), F32), small, small, small, small],
        compiler_params=pltpu.CompilerParams(
            dimension_semantics=("parallel", "parallel"), vmem_limit_bytes=VMEM_LIMIT),
        name="attn",
    )(ql, qi, wi, kt, c, ct, wuv_bd)


def _mix_kernel(x_ref, a_ref, u_ref, uh_ref, wp_ref, ps_ref, wo_ref, o_ref):
    ts = u_ref.shape[0]
    i = pl.program_id(1)
    halo = jnp.where(i == 0, 0.0, uh_ref[...])
    ext = jnp.concatenate([halo, u_ref[...]], axis=0)
    t = i * ts + lax.broadcasted_iota(jnp.int32, (ts, 1), 0)
    mixed = []
    for g, win in enumerate(POOL_WINDOWS):
        e = ext[:, g * POOL_GROUP:(g + 1) * POOL_GROUP]
        acc = e
        span = 1
        while span < win:
            acc = acc + pltpu.roll(acc, span, axis=0)
            span *= 2
        count = jnp.minimum(t + 1, win).astype(F32)
        pooled = acc[POOL_HALO:] / count - e[POOL_HALO:]
        mixed.append(jnp.dot(pooled.astype(BF16), wp_ref[g], preferred_element_type=F32))
    b = (jnp.concatenate(mixed, axis=1) * ps_ref[...]).astype(BF16)
    ab = jnp.concatenate([a_ref[...], b], axis=1)
    o_ref[...] = x_ref[...] + jnp.dot(ab, wo_ref[...], preferred_element_type=F32)


def _mix(x, a, u, w_pool, pool_scale, w_o, ts):
    B, S, D = x.shape
    grid = (B, S // ts)
    hb = ts // POOL_HALO
    return pl.pallas_call(
        _mix_kernel,
        grid=grid,
        in_specs=[
            pl.BlockSpec((None, ts, D), lambda b, i: (b, i, 0)),
            pl.BlockSpec((None, ts, COL_Q), lambda b, i: (b, i, 0)),
            pl.BlockSpec((None, ts, POOL_WIDTH), lambda b, i: (b, i, 0)),
            pl.BlockSpec((None, POOL_HALO, POOL_WIDTH), lambda b, i: (b, jnp.maximum(i * hb - 1, 0), 0)),
            pl.BlockSpec((len(POOL_WINDOWS), POOL_GROUP, POOL_GROUP), lambda b, i: (0, 0, 0)),
            pl.BlockSpec((1, POOL_WIDTH), lambda b, i: (0, 0)),
            pl.BlockSpec((COL_Q + POOL_WIDTH, D), lambda b, i: (0, 0)),
        ],
        out_specs=pl.BlockSpec((None, ts, D), lambda b, i: (b, i, 0)),
        out_shape=jax.ShapeDtypeStruct((B, S, D), F32),
        compiler_params=pltpu.CompilerParams(
            dimension_semantics=("parallel", "parallel"), vmem_limit_bytes=VMEM_LIMIT),
        name="mix",
    )(x, a, u, u, w_pool, pool_scale, w_o)


def _ffn_kernel(x_ref, xh_ref, g_ref, wup_ref, cw_ref, cb_ref, wdn_ref, gf_ref, o_ref, *, final_norm):
    tm, D = x_ref.shape
    n_chunks, _, fc2 = wup_ref.shape
    fc = fc2 // 2
    i = pl.program_id(1)
    x = x_ref[...]
    xe = jnp.concatenate([xh_ref[...], x], axis=0)
    h = _rms(xe, g_ref[...])
    row = lax.broadcasted_iota(jnp.int32, (CONV_HALO + tm, 1), 0)
    h = jnp.where(jnp.logical_and(i == 0, row < CONV_HALO), 0.0, h).astype(BF16)

    def chunk(k, acc):
        a = jnp.dot(h, wup_ref[k], preferred_element_type=F32)
        cw = cw_ref[k]
        conv = cb_ref[k] + cw[2:3] * a
        for jj in range(CONV_WIDTH - 1):
            conv = conv + cw[jj:jj + 1] * pltpu.roll(a, CONV_WIDTH - 1 - jj, axis=0)
        conv = conv[CONV_HALO:]
        gate = conv[:, :fc]
        act = gate * (1.0 / (1.0 + jnp.exp(-gate))) * conv[:, fc:]
        return acc + jnp.dot(act.astype(BF16), wdn_ref[k], preferred_element_type=F32)

    y = x + lax.fori_loop(0, n_chunks, chunk, jnp.zeros((tm, D), F32))
    if final_norm:
        y = _rms(y, gf_ref[...])
    o_ref[...] = y


def _ffn(x, g, wup, cw, cb, wdn, g_final, tm, final_norm):
    B, S, D = x.shape
    grid = (B, S // tm)
    hb = tm // CONV_HALO
    const = lambda *shape: pl.BlockSpec(shape, lambda b, i: (0,) * len(shape))
    return pl.pallas_call(
        functools.partial(_ffn_kernel, final_norm=final_norm),
        grid=grid,
        in_specs=[
            pl.BlockSpec((None, tm, D), lambda b, i: (b, i, 0)),
            pl.BlockSpec((None, CONV_HALO, D), lambda b, i: (b, jnp.maximum(i * hb - 1, 0), 0)),
            const(1, D),
            const(*wup.shape),
            const(*cw.shape),
            const(*cb.shape),
            const(*wdn.shape),
            const(1, D),
        ],
        out_specs=pl.BlockSpec((None, tm, D), lambda b, i: (b, i, 0)),
        out_shape=jax.ShapeDtypeStruct((B, S, D), F32),
        compiler_params=pltpu.CompilerParams(
            dimension_semantics=("parallel", "parallel"), vmem_limit_bytes=VMEM_LIMIT),
        name="ffn",
    )(x, x, g, wup, cw, cb, wdn, g_final)


def _ffn_chunk(d_ff):
    for fc in (256, 128):
        if d_ff % fc == 0:
            return fc
    raise ValueError(f"unsupported FFN width {d_ff}")


def kernel(x, g_mix, w_in, g_kv, w_uk, w_uv, w_pool, pool_scale, w_o, g_ffn, w_up, conv_w, conv_b, w_down, g_final):
    B, S, D = x.shape
    depth = g_mix.shape[0]
    d_ff = w_down.shape[1]
    fc = _ffn_chunk(d_ff)
    n_chunks = d_ff // fc
    n_top = min(TOPK_MAX, S // 4)
    tm = min(512, S)
    qb = min(128, S)
    assert S % tm == 0 and S % qb == 0 and qb % CHUNK == 0
    assert w_in.shape[2] == COL_Q + KV_LATENT + COL_QI + IDX_HEAD_DIM + N_IDX_HEADS + POOL_WIDTH

    s_q = COL_Q
    s_c = s_q + KV_LATENT
    s_qi = s_c + COL_QI
    s_ki = s_qi + IDX_HEAD_DIM
    s_wi = s_ki + N_IDX_HEADS
    row = lambda v: v.reshape(1, -1).astype(F32)

    for l in range(depth):
        w = w_in[l]
        w_cat = jnp.concatenate(
            [w[:, :s_q], w[:, s_q:s_c], w[:, s_c:s_qi], w[:, s_wi:], w[:, s_qi:s_ki], w[:, s_ki:s_wi],
             jnp.zeros((D, TAIL - IDX_HEAD_DIM - N_IDX_HEADS), w.dtype)], axis=1).astype(BF16)
        wuv_bd = jnp.zeros((N_ATT_HEADS, KV_LATENT, N_ATT_HEADS, ATT_HEAD_DIM), F32)
        wuv_bd = wuv_bd.at[jnp.arange(N_ATT_HEADS), :, jnp.arange(N_ATT_HEADS), :].set(w_uv[l])
        wuv_bd = wuv_bd.reshape(N_ATT_HEADS * KV_LATENT, COL_Q).astype(BF16)
        pair = lambda m: jnp.concatenate(
            [m[..., :d_ff].reshape(m.shape[:-1] + (n_chunks, fc)),
             m[..., d_ff:].reshape(m.shape[:-1] + (n_chunks, fc))], axis=-1)
        wup = jnp.moveaxis(pair(w_up[l]), 1, 0).astype(BF16)
        cw = jnp.moveaxis(pair(conv_w[l]), 1, 0).astype(F32)
        cb = pair(conv_b[l]).reshape(n_chunks, 1, 2 * fc).astype(F32)
        wdn = w_down[l].reshape(n_chunks, fc, D).astype(BF16)

        ql, c, ct, qi, kt, wi, u = _proj(x, row(g_mix[l]), w_cat, w_uk[l].astype(BF16), row(g_kv[l]), tm)
        a = _attn(ql, qi, wi, kt, c, ct, wuv_bd, qb, n_top)
        x = _mix(x, a, u, w_pool[l].astype(BF16), row(pool_scale[l]), w_o[l].astype(BF16), tm)
        x = _ffn(x, row(g_ffn[l]), wup, cw, cb, wdn, row(g_final), tm, final_norm=(l == depth - 1))
    return x
```

```python
import functools

import jax
import jax.numpy as jnp
from jax import lax
from jax.experimental import pallas as pl
from jax.experimental.pallas import tpu as pltpu

CHUNK = 64
N_ATT_HEADS = 8
ATT_HEAD_DIM = 64
KV_LATENT = 128
N_IDX_HEADS = 8
IDX_HEAD_DIM = 64
TOPK_MAX = 256
POOL_WINDOWS = (2, 4, 8, 16)
POOL_GROUP = 128
CONV_WIDTH = 3
EPS = 1e-6

COL_Q = N_ATT_HEADS * ATT_HEAD_DIM
COL_QI = N_IDX_HEADS * IDX_HEAD_DIM
POOL_WIDTH = POOL_GROUP * len(POOL_WINDOWS)
TAIL = 128
POOL_HALO = 16
CONV_HALO = 8
MASKED_DIST = 1e30
FIRST_BISECT_STEPS = 12
MORE_BISECT_STEPS = 3
TIE_BISECT_STEPS = 8
MAX_BISECT_ROUNDS = 100
KEY_EXTENT_STEP = 512
ONES_COLS = 128
LOG2E = 1.4426950408889634
VMEM_LIMIT = 56 * 1024 * 1024

F32 = jnp.float32
BF16 = jnp.bfloat16


def _rms(x, g):
    return x * lax.rsqrt(jnp.mean(x * x, axis=-1, keepdims=True) + EPS) * g


def _proj_kernel(x_ref, g_ref, w_ref, wuk_ref, gkv_ref,
                 ql_ref, cx_ref, ct_ref, qi_ref, kt_ref, wi_ref, u_ref):
    h = _rms(x_ref[...], g_ref[...]).astype(BF16)
    p = jnp.dot(h, w_ref[...], preferred_element_type=F32)
    o_c = COL_Q
    o_qi = o_c + KV_LATENT
    o_u = o_qi + COL_QI
    o_t = o_u + POOL_WIDTH
    q = p[:, :o_c].astype(BF16)
    for hd in range(N_ATT_HEADS):
        ql = jnp.dot(q[:, hd * ATT_HEAD_DIM:(hd + 1) * ATT_HEAD_DIM], wuk_ref[hd],
                     preferred_element_type=F32)
        ql_ref[hd] = (ql * (ATT_HEAD_DIM ** -0.5 * LOG2E)).astype(BF16)
    c = _rms(p[:, o_c:o_qi], gkv_ref[...])
    ones_col = jnp.where(lax.broadcasted_iota(jnp.int32, (c.shape[0], ONES_COLS), 1) == 0, 1.0, 0.0)
    cx_ref[...] = jnp.concatenate([c, ones_col], axis=1).astype(BF16)
    ct_ref[...] = c.T.astype(BF16)
    qi_ref[...] = p[:, o_qi:o_u].astype(BF16)
    u_ref[...] = p[:, o_u:o_t]
    tail = p[:, o_t:o_t + TAIL]
    kt_ref[...] = tail.T[:IDX_HEAD_DIM].astype(BF16)
    wi_ref[...] = tail[:, IDX_HEAD_DIM:IDX_HEAD_DIM + N_IDX_HEADS] * (
        (N_IDX_HEADS ** -0.5) * (IDX_HEAD_DIM ** -0.5))


def _proj(x, g, w_cat, wuk, gkv, tm):
    B, S, D = x.shape
    ncol = w_cat.shape[1]
    grid = (B, S // tm)
    const = lambda *shape: pl.BlockSpec(shape, lambda b, i: (0,) * len(shape))
    return pl.pallas_call(
        _proj_kernel,
        grid=grid,
        in_specs=[
            pl.BlockSpec((None, tm, D), lambda b, i: (b, i, 0)),
            const(1, D),
            const(D, ncol),
            const(N_ATT_HEADS, ATT_HEAD_DIM, KV_LATENT),
            const(1, KV_LATENT),
        ],
        out_specs=[
            pl.BlockSpec((None, N_ATT_HEADS, tm, KV_LATENT), lambda b, i: (b, 0, i, 0)),
            pl.BlockSpec((None, tm, KV_LATENT + ONES_COLS), lambda b, i: (b, i, 0)),
            pl.BlockSpec((None, KV_LATENT, tm), lambda b, i: (b, 0, i)),
            pl.BlockSpec((None, tm, COL_QI), lambda b, i: (b, i, 0)),
            pl.BlockSpec((None, IDX_HEAD_DIM, tm), lambda b, i: (b, 0, i)),
            pl.BlockSpec((None, tm, N_IDX_HEADS), lambda b, i: (b, i, 0)),
            pl.BlockSpec((None, tm, POOL_WIDTH), lambda b, i: (b, i, 0)),
        ],
        out_shape=[
            jax.ShapeDtypeStruct((B, N_ATT_HEADS, S, KV_LATENT), BF16),
            jax.ShapeDtypeStruct((B, S, KV_LATENT + ONES_COLS), BF16),
            jax.ShapeDtypeStruct((B, KV_LATENT, S), BF16),
            jax.ShapeDtypeStruct((B, S, COL_QI), BF16),
            jax.ShapeDtypeStruct((B, IDX_HEAD_DIM, S), BF16),
            jax.ShapeDtypeStruct((B, S, N_IDX_HEADS), F32),
            jax.ShapeDtypeStruct((B, S, POOL_WIDTH), F32),
        ],
        compiler_params=pltpu.CompilerParams(
            dimension_semantics=("parallel", "parallel"), vmem_limit_bytes=VMEM_LIMIT),
        name="proj",
    )(x, g, w_cat, wuk, gkv)


def _any_row(mask):
    return jnp.max(jnp.where(mask, 1.0, 0.0)) > 0.0


def _attend(extent, j, ql_ref, qi_ref, wi_ref, kt_ref, cx_ref, ct_ref, wuv_ref, a_ref, sm_ref, dm_ref, n_top):
    qb = qi_ref.shape[0]
    E = extent

    qi = qi_ref[...]
    kt = kt_ref[:, :E]
    wi = wi_ref[...]
    score = None
    for i in range(N_IDX_HEADS):
        rel = jnp.dot(qi[:, i * IDX_HEAD_DIM:(i + 1) * IDX_HEAD_DIM], kt,
                      preferred_element_type=F32)
        term = wi[:, i:i + 1] * jnp.maximum(rel, 0.0)
        score = term if score is None else score + term

    qpos = j * qb + lax.broadcasted_iota(jnp.int32, (qb, 1), 0)
    kpos = lax.broadcasted_iota(jnp.int32, (1, E), 1)
    qchunk = qpos // CHUNK
    adm = (kpos // CHUNK) <= qchunk
    sm_ref[:, :E] = jnp.where(adm, score, -jnp.inf)

    n_adm = ((qchunk + 1) * CHUNK).astype(F32)
    k_sel = jnp.minimum(n_adm, float(n_top))
    pos_inf = jnp.full((qb, 1), jnp.inf, F32)
    row_min = jnp.min(jnp.where(adm, score, jnp.inf), axis=1, keepdims=True)
    row_max = jnp.max(sm_ref[:, :E], axis=1, keepdims=True)
    state = (row_min, row_max + (jnp.abs(row_max) * (2.0 ** -20) + 1e-30), n_adm, jnp.zeros_like(n_adm))

    def bisect(n, st):
        lo, hi, clo, chi = st
        for _ in range(n):
            mid = lo + 0.5 * (hi - lo)
            cnt = jnp.sum(jnp.where(sm_ref[:, :E] >= mid, 1.0, 0.0), axis=1, keepdims=True)
            ge = cnt >= k_sel
            lo, hi = jnp.where(ge, mid, lo), jnp.where(ge, hi, mid)
            clo, chi = jnp.where(ge, cnt, clo), jnp.where(ge, chi, cnt)
        return lo, hi, clo, chi

    def in_range_ends(st):
        s = sm_ref[:, :E]
        v_lo = jnp.min(jnp.where(s >= st[0], s, jnp.inf), axis=1, keepdims=True)
        v_hi = jnp.max(jnp.where(s < st[1], s, -jnp.inf), axis=1, keepdims=True)
        return v_lo, v_hi

    def peel(st):
        lo, hi, clo, chi = st
        v_lo, v_hi = in_range_ends(st)
        done = clo == k_sel
        one_short = jnp.logical_and(jnp.logical_not(done), k_sel - chi == 1.0)
        one_over = jnp.logical_and(jnp.logical_not(jnp.logical_or(done, one_short)), clo - k_sel == 1.0)
        settled = jnp.logical_or(jnp.logical_or(done, one_short), jnp.logical_or(one_over, v_lo == v_hi))
        return (jnp.where(one_short, v_hi, lo), jnp.where(one_over, v_lo, pos_inf),
                _any_row(jnp.logical_not(settled)))

    need_sel = (j + 1) * qb > n_top
    state = lax.cond(need_sel, functools.partial(bisect, FIRST_BISECT_STEPS), lambda st: st, state)

    def refine(carry):
        _, rounds, st, _, _ = carry
        lo_sel, drop, is_open = peel(st)
        st = lax.cond(is_open, functools.partial(bisect, MORE_BISECT_STEPS), lambda s: s, st)
        return is_open.astype(jnp.int32), rounds + 1, st, lo_sel, drop

    undecided = jnp.logical_and(need_sel, _any_row(state[2] != k_sel))
    _, _, state, lo_sel, drop = lax.while_loop(
        lambda c: jnp.logical_and(c[0] > 0, c[1] < MAX_BISECT_ROUNDS), refine,
        (undecided.astype(jnp.int32), jnp.int32(0), state, state[0], pos_inf))

    s = sm_ref[:, :E]
    sel = jnp.logical_and(s >= lo_sel, s != drop)
    n_sel = jnp.sum(jnp.where(sel, 1.0, 0.0), axis=1, keepdims=True)
    exact_needed = _any_row(n_sel != k_sel)
    dist = jnp.abs(qpos - kpos).astype(F32)
    kposf = kpos.astype(F32)

    @pl.when(jnp.logical_not(exact_needed))
    def _():
        dm_ref[:, :E] = jnp.where(sel, dist, MASKED_DIST)

    @pl.when(exact_needed)
    def _():
        def unresolved(st):
            v_lo, v_hi = in_range_ends(st)
            return _any_row(jnp.logical_and(st[2] != k_sel, v_lo != v_hi)).astype(jnp.int32)

        def tighten(carry):
            _, rounds, st = carry
            st = bisect(TIE_BISECT_STEPS, st)
            return unresolved(st), rounds + 1, st

        _, _, (lo, hi, _, chi) = lax.while_loop(
            lambda c: jnp.logical_and(c[0] > 0, c[1] < MAX_BISECT_ROUNDS), tighten,
            (unresolved(state), jnp.int32(0), state))
        sc = sm_ref[:, :E]
        in_range = jnp.logical_and(sc >= lo, sc < hi)
        need = k_sel - chi

        def step(_, bounds):
            lo_i, hi_i = bounds
            mid_i = jnp.floor(0.5 * (lo_i + hi_i))
            cnt = jnp.sum(jnp.where(jnp.logical_and(in_range, kposf <= mid_i), 1.0, 0.0),
                          axis=1, keepdims=True)
            ok = cnt >= need
            return jnp.where(ok, lo_i, mid_i), jnp.where(ok, mid_i, hi_i)

        n_steps = max(1, (E - 1).bit_length()) + 1
        init = (jnp.full((qb, 1), -1.0, F32), jnp.full((qb, 1), float(E - 1), F32))
        cut = lax.fori_loop(0, n_steps, step, init)[1]
        sel_x = jnp.logical_or(sc >= hi, jnp.logical_and(in_range, kposf <= cut))
        dm_ref[:, :E] = jnp.where(sel_x, dist, MASKED_DIST)

    cx = cx_ref[:E]
    ct = ct_ref[:, :E]
    o_heads = []
    for hd in range(N_ATT_HEADS):
        slope = LOG2E * 2.0 ** (-8.0 * (hd + 1) / N_ATT_HEADS)
        logits = jnp.dot(ql_ref[hd], ct, preferred_element_type=F32) - slope * dm_ref[:, :E]
        p = jnp.exp2(logits - jnp.max(logits, axis=1, keepdims=True))
        ox = jnp.dot(p.astype(BF16), cx, preferred_element_type=F32)
        o_heads.append((ox[:, :KV_LATENT] / ox[:, KV_LATENT:KV_LATENT + 1]).astype(BF16))
    o_lat = jnp.concatenate(o_heads, axis=1)
    a_ref[...] = jnp.dot(o_lat, wuv_ref[...], preferred_element_type=F32).astype(BF16)


def _attn_kernel(ql_ref, qi_ref, wi_ref, kt_ref, cx_ref, ct_ref, wuv_ref, a_ref, sm_ref, dm_ref,
                 *, n_top, extents):
    qb = qi_ref.shape[0]
    j = pl.program_id(1)
    n_keys = (j + 1) * qb
    prev = 0
    for extent in extents:
        @pl.when(jnp.logical_and(n_keys > prev, n_keys <= extent))
        def _(extent=extent):
            _attend(extent, j, ql_ref, qi_ref, wi_ref, kt_ref, cx_ref, ct_ref, wuv_ref, a_ref,
                    sm_ref, dm_ref, n_top)
        prev = extent


def _attn(ql, qi, wi, kt, cx, ct, wuv_bd, qb, n_top):
    B, H, S, dc = ql.shape
    grid = (B, S // qb)
    step = min(KEY_EXTENT_STEP, S)
    extents = tuple(range(step, S + 1, step))
    return pl.pallas_call(
        functools.partial(_attn_kernel, n_top=n_top, extents=extents),
        grid=grid,
        in_specs=[
            pl.BlockSpec((None, H, qb, dc), lambda b, j: (b, 0, j, 0)),
            pl.BlockSpec((None, qb, COL_QI), lambda b, j: (b, j, 0)),
            pl.BlockSpec((None, qb, N_IDX_HEADS), lambda b, j: (b, j, 0)),
            pl.BlockSpec((None, IDX_HEAD_DIM, S), lambda b, j: (b, 0, 0)),
            pl.BlockSpec((None, S, dc + ONES_COLS), lambda b, j: (b, 0, 0)),
            pl.BlockSpec((None, dc, S), lambda b, j: (b, 0, 0)),
            pl.BlockSpec((H * dc, COL_Q), lambda b, j: (0, 0)),
        ],
        out_specs=pl.BlockSpec((None, qb, COL_Q), lambda b, j: (b, j, 0)),
        out_shape=jax.ShapeDtypeStruct((B, S, COL_Q), BF16),
        scratch_shapes=[pltpu.VMEM((qb, S), F32), pltpu.VMEM((qb, S), F32)],
        compiler_params=pltpu.CompilerParams(
            dimension_semantics=("parallel", "parallel"), vmem_limit_bytes=VMEM_LIMIT),
        name="attn",
    )(ql, qi, wi, kt, cx, ct, wuv_bd)


def _mix_kernel(x_ref, a_ref, u_ref, uh_ref, wp_ref, ps_ref, wo_ref, o_ref):
    ts = u_ref.shape[0]
    i = pl.program_id(1)
    halo = jnp.where(i == 0, 0.0, uh_ref[...])
    ext = jnp.concatenate([halo, u_ref[...]], axis=0)
    t = i * ts + lax.broadcasted_iota(jnp.int32, (ts, 1), 0)
    mixed = []
    for g, win in enumerate(POOL_WINDOWS):
        e = ext[:, g * POOL_GROUP:(g + 1) * POOL_GROUP]
        acc = e
        span = 1
        while span < win:
            acc = acc + pltpu.roll(acc, span, axis=0)
            span *= 2
        count = jnp.minimum(t + 1, win).astype(F32)
        pooled = acc[POOL_HALO:] / count - e[POOL_HALO:]
        mixed.append(jnp.dot(pooled.astype(BF16), wp_ref[g], preferred_element_type=F32))
    b = (jnp.concatenate(mixed, axis=1) * ps_ref[...]).astype(BF16)
    ab = jnp.concatenate([a_ref[...], b], axis=1)
    o_ref[...] = x_ref[...] + jnp.dot(ab, wo_ref[...], preferred_element_type=F32)


def _mix(x, a, u, w_pool, pool_scale, w_o, ts):
    B, S, D = x.shape
    grid = (B, S // ts)
    hb = ts // POOL_HALO
    return pl.pallas_call(
        _mix_kernel,
        grid=grid,
        in_specs=[
            pl.BlockSpec((None, ts, D), lambda b, i: (b, i, 0)),
            pl.BlockSpec((None, ts, COL_Q), lambda b, i: (b, i, 0)),
            pl.BlockSpec((None, ts, POOL_WIDTH), lambda b, i: (b, i, 0)),
            pl.BlockSpec((None, POOL_HALO, POOL_WIDTH), lambda b, i: (b, jnp.maximum(i * hb - 1, 0), 0)),
            pl.BlockSpec((len(POOL_WINDOWS), POOL_GROUP, POOL_GROUP), lambda b, i: (0, 0, 0)),
            pl.BlockSpec((1, POOL_WIDTH), lambda b, i: (0, 0)),
            pl.BlockSpec((COL_Q + POOL_WIDTH, D), lambda b, i: (0, 0)),
        ],
        out_specs=pl.BlockSpec((None, ts, D), lambda b, i: (b, i, 0)),
        out_shape=jax.ShapeDtypeStruct((B, S, D), F32),
        compiler_params=pltpu.CompilerParams(
            dimension_semantics=("parallel", "parallel"), vmem_limit_bytes=VMEM_LIMIT),
        name="mix",
    )(x, a, u, u, w_pool, pool_scale, w_o)


def _ffn_kernel(x_ref, xh_ref, g_ref, wup_ref, cw_ref, cb_ref, wdn_ref, gf_ref, o_ref, *, final_norm):
    tm, D = x_ref.shape
    n_chunks, _, fc2 = wup_ref.shape
    fc = fc2 // 2
    i = pl.program_id(1)
    x = x_ref[...]
    xe = jnp.concatenate([xh_ref[...], x], axis=0)
    h = _rms(xe, g_ref[...])
    row = lax.broadcasted_iota(jnp.int32, (CONV_HALO + tm, 1), 0)
    h = jnp.where(jnp.logical_and(i == 0, row < CONV_HALO), 0.0, h).astype(BF16)

    def chunk(k, acc):
        a = jnp.dot(h, wup_ref[k], preferred_element_type=F32)
        cw = cw_ref[k]
        conv = cb_ref[k] + cw[2:3] * a
        for jj in range(CONV_WIDTH - 1):
            conv = conv + cw[jj:jj + 1] * pltpu.roll(a, CONV_WIDTH - 1 - jj, axis=0)
        conv = conv[CONV_HALO:]
        gate = conv[:, :fc]
        act = gate * (1.0 / (1.0 + jnp.exp(-gate))) * conv[:, fc:]
        return acc + jnp.dot(act.astype(BF16), wdn_ref[k], preferred_element_type=F32)

    y = x + lax.fori_loop(0, n_chunks, chunk, jnp.zeros((tm, D), F32), unroll=True)
    if final_norm:
        y = _rms(y, gf_ref[...])
    o_ref[...] = y


def _ffn(x, g, wup, cw, cb, wdn, g_final, tm, final_norm):
    B, S, D = x.shape
    grid = (B, S // tm)
    hb = tm // CONV_HALO
    const = lambda *shape: pl.BlockSpec(shape, lambda b, i: (0,) * len(shape))
    return pl.pallas_call(
        functools.partial(_ffn_kernel, final_norm=final_norm),
        grid=grid,
        in_specs=[
            pl.BlockSpec((None, tm, D), lambda b, i: (b, i, 0)),
            pl.BlockSpec((None, CONV_HALO, D), lambda b, i: (b, jnp.maximum(i * hb - 1, 0), 0)),
            const(1, D),
            const(*wup.shape),
            const(*cw.shape),
            const(*cb.shape),
            const(*wdn.shape),
            const(1, D),
        ],
        out_specs=pl.BlockSpec((None, tm, D), lambda b, i: (b, i, 0)),
        out_shape=jax.ShapeDtypeStruct((B, S, D), F32),
        compiler_params=pltpu.CompilerParams(
            dimension_semantics=("parallel", "parallel"), vmem_limit_bytes=VMEM_LIMIT),
        name="ffn",
    )(x, x, g, wup, cw, cb, wdn, g_final)


def _ffn_chunk(d_ff):
    for fc in (256, 128):
        if d_ff % fc == 0:
            return fc
    raise ValueError(f"unsupported FFN width {d_ff}")


def kernel(x, g_mix, w_in, g_kv, w_uk, w_uv, w_pool, pool_scale, w_o, g_ffn, w_up, conv_w, conv_b, w_down, g_final):
    B, S, D = x.shape
    depth = g_mix.shape[0]
    d_ff = w_down.shape[1]
    fc = _ffn_chunk(d_ff)
    n_chunks = d_ff // fc
    n_top = min(TOPK_MAX, S // 4)
    tm = min(512, S)
    qb = min(128, S)
    assert S % tm == 0 and S % qb == 0 and qb % CHUNK == 0
    assert w_in.shape[2] == COL_Q + KV_LATENT + COL_QI + IDX_HEAD_DIM + N_IDX_HEADS + POOL_WIDTH

    s_q = COL_Q
    s_c = s_q + KV_LATENT
    s_qi = s_c + COL_QI
    s_ki = s_qi + IDX_HEAD_DIM
    s_wi = s_ki + N_IDX_HEADS
    row = lambda v: v.reshape(1, -1).astype(F32)

    for l in range(depth):
        w = w_in[l]
        w_cat = jnp.concatenate(
            [w[:, :s_q], w[:, s_q:s_c], w[:, s_c:s_qi], w[:, s_wi:], w[:, s_qi:s_ki], w[:, s_ki:s_wi],
             jnp.zeros((D, TAIL - IDX_HEAD_DIM - N_IDX_HEADS), w.dtype)], axis=1).astype(BF16)
        wuv_bd = jnp.zeros((N_ATT_HEADS, KV_LATENT, N_ATT_HEADS, ATT_HEAD_DIM), F32)
        wuv_bd = wuv_bd.at[jnp.arange(N_ATT_HEADS), :, jnp.arange(N_ATT_HEADS), :].set(w_uv[l])
        wuv_bd = wuv_bd.reshape(N_ATT_HEADS * KV_LATENT, COL_Q).astype(BF16)
        pair = lambda m: jnp.concatenate(
            [m[..., :d_ff].reshape(m.shape[:-1] + (n_chunks, fc)),
             m[..., d_ff:].reshape(m.shape[:-1] + (n_chunks, fc))], axis=-1)
        wup = jnp.moveaxis(pair(w_up[l]), 1, 0).astype(BF16)
        cw = jnp.moveaxis(pair(conv_w[l]), 1, 0).astype(F32)
        cb = pair(conv_b[l]).reshape(n_chunks, 1, 2 * fc).astype(F32)
        wdn = w_down[l].reshape(n_chunks, fc, D).astype(BF16)

        ql, cx, ct, qi, kt, wi, u = _proj(x, row(g_mix[l]), w_cat, w_uk[l].astype(BF16), row(g_kv[l]), tm)
        a = _attn(ql, qi, wi, kt, cx, ct, wuv_bd, qb, n_top)
        x = _mix(x, a, u, w_pool[l].astype(BF16), row(pool_scale[l]), w_o[l].astype(BF16), tm)
        x = _ffn(x, row(g_ffn[l]), wup, cw, cb, wdn, row(g_final), tm, final_norm=(l == depth - 1))
    return x
```

```python
import functools

import jax
import jax.numpy as jnp
from jax import lax
from jax.experimental import pallas as pl
from jax.experimental.pallas import tpu as pltpu

CHUNK = 64
N_ATT_HEADS = 8
ATT_HEAD_DIM = 64
KV_LATENT = 128
N_IDX_HEADS = 8
IDX_HEAD_DIM = 64
TOPK_MAX = 256
POOL_WINDOWS = (2, 4, 8, 16)
POOL_GROUP = 128
CONV_WIDTH = 3
EPS = 1e-6

COL_Q = N_ATT_HEADS * ATT_HEAD_DIM
COL_QI = N_IDX_HEADS * IDX_HEAD_DIM
POOL_WIDTH = POOL_GROUP * len(POOL_WINDOWS)
TAIL = 128
POOL_HALO = 16
CONV_HALO = 8
MASKED_DIST = 1e30
FIRST_BISECT_STEPS = 12
MORE_BISECT_STEPS = 3
TIE_BISECT_STEPS = 8
MAX_BISECT_ROUNDS = 100
ONES_COLS = 128
LOG2E = 1.4426950408889634
VMEM_LIMIT = 56 * 1024 * 1024

F32 = jnp.float32
BF16 = jnp.bfloat16


def _rms(x, g):
    return x * lax.rsqrt(jnp.mean(x * x, axis=-1, keepdims=True) + EPS) * g


def _proj_kernel(x_ref, g_ref, w_ref, wuk_ref, gkv_ref,
                 ql_ref, cx_ref, ct_ref, qi_ref, kt_ref, wi_ref, u_ref):
    h = _rms(x_ref[...], g_ref[...]).astype(BF16)
    p = jnp.dot(h, w_ref[...], preferred_element_type=F32)
    o_c = COL_Q
    o_qi = o_c + KV_LATENT
    o_u = o_qi + COL_QI
    o_t = o_u + POOL_WIDTH
    q = p[:, :o_c].astype(BF16)
    for hd in range(N_ATT_HEADS):
        ql = jnp.dot(q[:, hd * ATT_HEAD_DIM:(hd + 1) * ATT_HEAD_DIM], wuk_ref[hd],
                     preferred_element_type=F32)
        ql_ref[hd] = (ql * (ATT_HEAD_DIM ** -0.5 * LOG2E)).astype(BF16)
    c = _rms(p[:, o_c:o_qi], gkv_ref[...])
    ones_col = jnp.where(lax.broadcasted_iota(jnp.int32, (c.shape[0], ONES_COLS), 1) == 0, 1.0, 0.0)
    cx_ref[...] = jnp.concatenate([c, ones_col], axis=1).astype(BF16)
    ct_ref[...] = c.T.astype(BF16)
    qi_ref[...] = p[:, o_qi:o_u].astype(BF16)
    u_ref[...] = p[:, o_u:o_t]
    tail = p[:, o_t:o_t + TAIL]
    kt_ref[...] = tail.T[:IDX_HEAD_DIM].astype(BF16)
    wi_ref[...] = tail[:, IDX_HEAD_DIM:IDX_HEAD_DIM + N_IDX_HEADS] * (
        (N_IDX_HEADS ** -0.5) * (IDX_HEAD_DIM ** -0.5))


def _proj(x, g, w_cat, wuk, gkv, tm):
    B, S, D = x.shape
    ncol = w_cat.shape[1]
    grid = (B, S // tm)
    const = lambda *shape: pl.BlockSpec(shape, lambda b, i: (0,) * len(shape))
    return pl.pallas_call(
        _proj_kernel,
        grid=grid,
        in_specs=[
            pl.BlockSpec((None, tm, D), lambda b, i: (b, i, 0)),
            const(1, D),
            const(D, ncol),
            const(N_ATT_HEADS, ATT_HEAD_DIM, KV_LATENT),
            const(1, KV_LATENT),
        ],
        out_specs=[
            pl.BlockSpec((None, N_ATT_HEADS, tm, KV_LATENT), lambda b, i: (b, 0, i, 0)),
            pl.BlockSpec((None, tm, KV_LATENT + ONES_COLS), lambda b, i: (b, i, 0)),
            pl.BlockSpec((None, KV_LATENT, tm), lambda b, i: (b, 0, i)),
            pl.BlockSpec((None, tm, COL_QI), lambda b, i: (b, i, 0)),
            pl.BlockSpec((None, IDX_HEAD_DIM, tm), lambda b, i: (b, 0, i)),
            pl.BlockSpec((None, tm, N_IDX_HEADS), lambda b, i: (b, i, 0)),
            pl.BlockSpec((None, tm, POOL_WIDTH), lambda b, i: (b, i, 0)),
        ],
        out_shape=[
            jax.ShapeDtypeStruct((B, N_ATT_HEADS, S, KV_LATENT), BF16),
            jax.ShapeDtypeStruct((B, S, KV_LATENT + ONES_COLS), BF16),
            jax.ShapeDtypeStruct((B, KV_LATENT, S), BF16),
            jax.ShapeDtypeStruct((B, S, COL_QI), BF16),
            jax.ShapeDtypeStruct((B, IDX_HEAD_DIM, S), BF16),
            jax.ShapeDtypeStruct((B, S, N_IDX_HEADS), F32),
            jax.ShapeDtypeStruct((B, S, POOL_WIDTH), F32),
        ],
        compiler_params=pltpu.CompilerParams(
            dimension_semantics=("parallel", "parallel"), vmem_limit_bytes=VMEM_LIMIT),
        name="proj",
    )(x, g, w_cat, wuk, gkv)


def _any_row(mask):
    return jnp.max(jnp.where(mask, 1.0, 0.0)) > 0.0


def _attend(extent, j, ql_ref, qi_ref, wi_ref, kt_ref, cx_ref, ct_ref, wuv_ref, a_ref, sm_ref, dm_ref, lg_ref,
            n_top):
    qb = qi_ref.shape[0]
    E = extent
    ct = ct_ref[:, :E]

    def logits_of(hd):
        return jnp.dot(ql_ref[hd], ct, preferred_element_type=F32)

    qi = qi_ref[...]
    kt = kt_ref[:, :E]
    wi = wi_ref[...]
    score = None
    for i in range(N_IDX_HEADS):
        rel = jnp.dot(qi[:, i * IDX_HEAD_DIM:(i + 1) * IDX_HEAD_DIM], kt,
                      preferred_element_type=F32)
        term = wi[:, i:i + 1] * jnp.maximum(rel, 0.0)
        score = term if score is None else score + term

    qpos = j * qb + lax.broadcasted_iota(jnp.int32, (qb, 1), 0)
    kpos = lax.broadcasted_iota(jnp.int32, (1, E), 1)
    qchunk = qpos // CHUNK
    adm = (kpos // CHUNK) <= qchunk
    dist = jnp.abs(qpos - kpos).astype(F32)

    if E <= n_top:
        dm_ref[:, :E] = jnp.where(adm, dist, MASKED_DIST)
        logits_from = logits_of
    else:
        sm_ref[:, :E] = jnp.where(adm, score, -jnp.inf)

        n_adm = ((qchunk + 1) * CHUNK).astype(F32)
        k_sel = jnp.minimum(n_adm, float(n_top))
        pos_inf = jnp.full((qb, 1), jnp.inf, F32)
        row_min = jnp.min(jnp.where(adm, score, jnp.inf), axis=1, keepdims=True)
        row_max = jnp.max(sm_ref[:, :E], axis=1, keepdims=True)
        state = (row_min, row_max + (jnp.abs(row_max) * (2.0 ** -20) + 1e-30), n_adm, jnp.zeros_like(n_adm))

        def bisect(n, st, between=None):
            lo, hi, clo, chi = st
            for step_i in range(n):
                mid = lo + 0.5 * (hi - lo)
                cnt = jnp.sum(jnp.where(sm_ref[:, :E] >= mid, 1.0, 0.0), axis=1, keepdims=True)
                ge = cnt >= k_sel
                lo, hi = jnp.where(ge, mid, lo), jnp.where(ge, hi, mid)
                clo, chi = jnp.where(ge, cnt, clo), jnp.where(ge, chi, cnt)
                if between is not None:
                    between(step_i)
            return lo, hi, clo, chi

        def in_range_ends(st):
            s = sm_ref[:, :E]
            v_lo = jnp.min(jnp.where(s >= st[0], s, jnp.inf), axis=1, keepdims=True)
            v_hi = jnp.max(jnp.where(s < st[1], s, -jnp.inf), axis=1, keepdims=True)
            return v_lo, v_hi

        def peel(st):
            lo, hi, clo, chi = st
            v_lo, v_hi = in_range_ends(st)
            done = clo == k_sel
            one_short = jnp.logical_and(jnp.logical_not(done), k_sel - chi == 1.0)
            one_over = jnp.logical_and(jnp.logical_not(jnp.logical_or(done, one_short)), clo - k_sel == 1.0)
            settled = jnp.logical_or(jnp.logical_or(done, one_short), jnp.logical_or(one_over, v_lo == v_hi))
            return (jnp.where(one_short, v_hi, lo), jnp.where(one_over, v_lo, pos_inf),
                    _any_row(jnp.logical_not(settled)))

        def park_logits(step_i):
            if step_i < N_ATT_HEADS:
                lg_ref[step_i, :, :E] = logits_of(step_i)

        state = bisect(FIRST_BISECT_STEPS, state, park_logits)
        logits_from = lambda hd: lg_ref[hd, :, :E]

        def refine(carry):
            _, rounds, st, _, _ = carry
            lo_sel, drop, is_open = peel(st)
            st = lax.cond(is_open, functools.partial(bisect, MORE_BISECT_STEPS), lambda s: s, st)
            return is_open.astype(jnp.int32), rounds + 1, st, lo_sel, drop

        undecided = _any_row(state[2] != k_sel)
        _, _, state, lo_sel, drop = lax.while_loop(
            lambda c: jnp.logical_and(c[0] > 0, c[1] < MAX_BISECT_ROUNDS), refine,
            (undecided.astype(jnp.int32), jnp.int32(0), state, state[0], pos_inf))

        s = sm_ref[:, :E]
        sel = jnp.logical_and(s >= lo_sel, s != drop)
        n_sel = jnp.sum(jnp.where(sel, 1.0, 0.0), axis=1, keepdims=True)
        exact_needed = _any_row(n_sel != k_sel)
        kposf = kpos.astype(F32)

        @pl.when(jnp.logical_not(exact_needed))
        def _():
            dm_ref[:, :E] = jnp.where(sel, dist, MASKED_DIST)

        @pl.when(exact_needed)
        def _():
            def unresolved(st):
                v_lo, v_hi = in_range_ends(st)
                return _any_row(jnp.logical_and(st[2] != k_sel, v_lo != v_hi)).astype(jnp.int32)

            def tighten(carry):
                _, rounds, st = carry
                st = bisect(TIE_BISECT_STEPS, st)
                return unresolved(st), rounds + 1, st

            _, _, (lo, hi, _, chi) = lax.while_loop(
                lambda c: jnp.logical_and(c[0] > 0, c[1] < MAX_BISECT_ROUNDS), tighten,
                (unresolved(state), jnp.int32(0), state))
            sc = sm_ref[:, :E]
            in_range = jnp.logical_and(sc >= lo, sc < hi)
            need = k_sel - chi

            def step(_, bounds):
                lo_i, hi_i = bounds
                mid_i = jnp.floor(0.5 * (lo_i + hi_i))
                cnt = jnp.sum(jnp.where(jnp.logical_and(in_range, kposf <= mid_i), 1.0, 0.0),
                              axis=1, keepdims=True)
                ok = cnt >= need
                return jnp.where(ok, lo_i, mid_i), jnp.where(ok, mid_i, hi_i)

            n_steps = max(1, (E - 1).bit_length()) + 1
            init = (jnp.full((qb, 1), -1.0, F32), jnp.full((qb, 1), float(E - 1), F32))
            cut = lax.fori_loop(0, n_steps, step, init)[1]
            sel_x = jnp.logical_or(sc >= hi, jnp.logical_and(in_range, kposf <= cut))
            dm_ref[:, :E] = jnp.where(sel_x, dist, MASKED_DIST)

    cx = cx_ref[:E]
    o_heads = []
    for hd in range(N_ATT_HEADS):
        slope = LOG2E * 2.0 ** (-8.0 * (hd + 1) / N_ATT_HEADS)
        logits = logits_from(hd) - slope * dm_ref[:, :E]
        p = jnp.exp2(logits - jnp.max(logits, axis=1, keepdims=True))
        ox = jnp.dot(p.astype(BF16), cx, preferred_element_type=F32)
        o_heads.append((ox[:, :KV_LATENT] / ox[:, KV_LATENT:KV_LATENT + 1]).astype(BF16))
    o_lat = jnp.concatenate(o_heads, axis=1)
    a_ref[...] = jnp.dot(o_lat, wuv_ref[...], preferred_element_type=F32).astype(BF16)


def _attn_kernel(ql_ref, qi_ref, wi_ref, kt_ref, cx_ref, ct_ref, wuv_ref, a_ref, sm_ref, dm_ref, lg_ref,
                 *, n_top):
    qb = qi_ref.shape[0]
    S = kt_ref.shape[1]
    j = pl.program_id(1)
    for jb in range(S // qb):
        @pl.when(j == jb)
        def _(jb=jb):
            _attend((jb + 1) * qb, j, ql_ref, qi_ref, wi_ref, kt_ref, cx_ref, ct_ref, wuv_ref, a_ref,
                    sm_ref, dm_ref, lg_ref, n_top)


def _attn(ql, qi, wi, kt, cx, ct, wuv_bd, qb, n_top):
    B, H, S, dc = ql.shape
    grid = (B, S // qb)
    return pl.pallas_call(
        functools.partial(_attn_kernel, n_top=n_top),
        grid=grid,
        in_specs=[
            pl.BlockSpec((None, H, qb, dc), lambda b, j: (b, 0, j, 0)),
            pl.BlockSpec((None, qb, COL_QI), lambda b, j: (b, j, 0)),
            pl.BlockSpec((None, qb, N_IDX_HEADS), lambda b, j: (b, j, 0)),
            pl.BlockSpec((None, IDX_HEAD_DIM, S), lambda b, j: (b, 0, 0)),
            pl.BlockSpec((None, S, dc + ONES_COLS), lambda b, j: (b, 0, 0)),
            pl.BlockSpec((None, dc, S), lambda b, j: (b, 0, 0)),
            pl.BlockSpec((H * dc, COL_Q), lambda b, j: (0, 0)),
        ],
        out_specs=pl.BlockSpec((None, qb, COL_Q), lambda b, j: (b, j, 0)),
        out_shape=jax.ShapeDtypeStruct((B, S, COL_Q), BF16),
        scratch_shapes=[pltpu.VMEM((qb, S), F32), pltpu.VMEM((qb, S), F32),
                        pltpu.VMEM((H, qb, S), F32)],
        compiler_params=pltpu.CompilerParams(
            dimension_semantics=("parallel", "parallel"), vmem_limit_bytes=VMEM_LIMIT),
        name="attn",
    )(ql, qi, wi, kt, cx, ct, wuv_bd)


def _mix_kernel(x_ref, a_ref, u_ref, uh_ref, wp_ref, ps_ref, wo_ref, o_ref):
    ts = u_ref.shape[0]
    i = pl.program_id(1)
    halo = jnp.where(i == 0, 0.0, uh_ref[...])
    ext = jnp.concatenate([halo, u_ref[...]], axis=0)
    t = i * ts + lax.broadcasted_iota(jnp.int32, (ts, 1), 0)
    mixed = []
    for g, win in enumerate(POOL_WINDOWS):
        e = ext[:, g * POOL_GROUP:(g + 1) * POOL_GROUP]
        acc = e
        span = 1
        while span < win:
            acc = acc + pltpu.roll(acc, span, axis=0)
            span *= 2
        count = jnp.minimum(t + 1, win).astype(F32)
        pooled = acc[POOL_HALO:] / count - e[POOL_HALO:]
        mixed.append(jnp.dot(pooled.astype(BF16), wp_ref[g], preferred_element_type=F32))
    b = (jnp.concatenate(mixed, axis=1) * ps_ref[...]).astype(BF16)
    ab = jnp.concatenate([a_ref[...], b], axis=1)
    o_ref[...] = x_ref[...] + jnp.dot(ab, wo_ref[...], preferred_element_type=F32)


def _mix(x, a, u, w_pool, pool_scale, w_o, ts):
    B, S, D = x.shape
    grid = (B, S // ts)
    hb = ts // POOL_HALO
    return pl.pallas_call(
        _mix_kernel,
        grid=grid,
        in_specs=[
            pl.BlockSpec((None, ts, D), lambda b, i: (b, i, 0)),
            pl.BlockSpec((None, ts, COL_Q), lambda b, i: (b, i, 0)),
            pl.BlockSpec((None, ts, POOL_WIDTH), lambda b, i: (b, i, 0)),
            pl.BlockSpec((None, POOL_HALO, POOL_WIDTH), lambda b, i: (b, jnp.maximum(i * hb - 1, 0), 0)),
            pl.BlockSpec((len(POOL_WINDOWS), POOL_GROUP, POOL_GROUP), lambda b, i: (0, 0, 0)),
            pl.BlockSpec((1, POOL_WIDTH), lambda b, i: (0, 0)),
            pl.BlockSpec((COL_Q + POOL_WIDTH, D), lambda b, i: (0, 0)),
        ],
        out_specs=pl.BlockSpec((None, ts, D), lambda b, i: (b, i, 0)),
        out_shape=jax.ShapeDtypeStruct((B, S, D), F32),
        compiler_params=pltpu.CompilerParams(
            dimension_semantics=("parallel", "parallel"), vmem_limit_bytes=VMEM_LIMIT),
        name="mix",
    )(x, a, u, u, w_pool, pool_scale, w_o)


def _ffn_kernel(x_ref, xh_ref, g_ref, wup_ref, cw_ref, cb_ref, wdn_ref, gf_ref, o_ref, act_ref, *, final_norm):
    tm, D = x_ref.shape
    n_chunks, _, fc2 = wup_ref.shape
    fc = fc2 // 2
    i = pl.program_id(1)
    x = x_ref[...]
    xe = jnp.concatenate([xh_ref[...], x], axis=0)
    h = _rms(xe, g_ref[...])
    row = lax.broadcasted_iota(jnp.int32, (CONV_HALO + tm, 1), 0)
    h = jnp.where(jnp.logical_and(i == 0, row < CONV_HALO), 0.0, h).astype(BF16)

    for k in range(n_chunks):
        a = jnp.dot(h, wup_ref[k], preferred_element_type=F32)
        cw = cw_ref[k]
        conv = cb_ref[k] + cw[2:3] * a
        for jj in range(CONV_WIDTH - 1):
            conv = conv + cw[jj:jj + 1] * pltpu.roll(a, CONV_WIDTH - 1 - jj, axis=0)
        conv = conv[CONV_HALO:]
        gate = conv[:, :fc]
        act = gate * (1.0 / (1.0 + jnp.exp(-gate))) * conv[:, fc:]
        act_ref[:, k * fc:(k + 1) * fc] = act.astype(BF16)

    y = x + jnp.dot(act_ref[...], wdn_ref[...], preferred_element_type=F32)
    if final_norm:
        y = _rms(y, gf_ref[...])
    o_ref[...] = y


def _ffn(x, g, wup, cw, cb, wdn, g_final, tm, final_norm):
    B, S, D = x.shape
    grid = (B, S // tm)
    hb = tm // CONV_HALO
    const = lambda *shape: pl.BlockSpec(shape, lambda b, i: (0,) * len(shape))
    return pl.pallas_call(
        functools.partial(_ffn_kernel, final_norm=final_norm),
        grid=grid,
        in_specs=[
            pl.BlockSpec((None, tm, D), lambda b, i: (b, i, 0)),
            pl.BlockSpec((None, CONV_HALO, D), lambda b, i: (b, jnp.maximum(i * hb - 1, 0), 0)),
            const(1, D),
            const(*wup.shape),
            const(*cw.shape),
            const(*cb.shape),
            const(*wdn.shape),
            const(1, D),
        ],
        out_specs=pl.BlockSpec((None, tm, D), lambda b, i: (b, i, 0)),
        out_shape=jax.ShapeDtypeStruct((B, S, D), F32),
        scratch_shapes=[pltpu.VMEM((tm, wdn.shape[0]), BF16)],
        compiler_params=pltpu.CompilerParams(
            dimension_semantics=("parallel", "parallel"), vmem_limit_bytes=VMEM_LIMIT),
        name="ffn",
    )(x, x, g, wup, cw, cb, wdn, g_final)


def _ffn_chunk(d_ff):
    for fc in (256, 128):
        if d_ff % fc == 0:
            return fc
    raise ValueError(f"unsupported FFN width {d_ff}")


def kernel(x, g_mix, w_in, g_kv, w_uk, w_uv, w_pool, pool_scale, w_o, g_ffn, w_up, conv_w, conv_b, w_down, g_final):
    B, S, D = x.shape
    depth = g_mix.shape[0]
    d_ff = w_down.shape[1]
    fc = _ffn_chunk(d_ff)
    n_chunks = d_ff // fc
    n_top = min(TOPK_MAX, S // 4)
    tm = min(512, S)
    qb = min(256, S)
    assert S % tm == 0 and S % qb == 0 and qb % CHUNK == 0
    assert w_in.shape[2] == COL_Q + KV_LATENT + COL_QI + IDX_HEAD_DIM + N_IDX_HEADS + POOL_WIDTH

    s_q = COL_Q
    s_c = s_q + KV_LATENT
    s_qi = s_c + COL_QI
    s_ki = s_qi + IDX_HEAD_DIM
    s_wi = s_ki + N_IDX_HEADS
    row = lambda v: v.reshape(1, -1).astype(F32)

    for l in range(depth):
        w = w_in[l]
        w_cat = jnp.concatenate(
            [w[:, :s_q], w[:, s_q:s_c], w[:, s_c:s_qi], w[:, s_wi:], w[:, s_qi:s_ki], w[:, s_ki:s_wi],
             jnp.zeros((D, TAIL - IDX_HEAD_DIM - N_IDX_HEADS), w.dtype)], axis=1).astype(BF16)
        wuv_bd = jnp.zeros((N_ATT_HEADS, KV_LATENT, N_ATT_HEADS, ATT_HEAD_DIM), F32)
        wuv_bd = wuv_bd.at[jnp.arange(N_ATT_HEADS), :, jnp.arange(N_ATT_HEADS), :].set(w_uv[l])
        wuv_bd = wuv_bd.reshape(N_ATT_HEADS * KV_LATENT, COL_Q).astype(BF16)
        pair = lambda m: jnp.concatenate(
            [m[..., :d_ff].reshape(m.shape[:-1] + (n_chunks, fc)),
             m[..., d_ff:].reshape(m.shape[:-1] + (n_chunks, fc))], axis=-1)
        wup = jnp.moveaxis(pair(w_up[l]), 1, 0).astype(BF16)
        cw = jnp.moveaxis(pair(conv_w[l]), 1, 0).astype(F32)
        cb = pair(conv_b[l]).reshape(n_chunks, 1, 2 * fc).astype(F32)
        wdn = w_down[l].astype(BF16)

        ql, cx, ct, qi, kt, wi, u = _proj(x, row(g_mix[l]), w_cat, w_uk[l].astype(BF16), row(g_kv[l]), tm)
        a = _attn(ql, qi, wi, kt, cx, ct, wuv_bd, qb, n_top)
        x = _mix(x, a, u, w_pool[l].astype(BF16), row(pool_scale[l]), w_o[l].astype(BF16), tm)
        x = _ffn(x, row(g_ffn[l]), wup, cw, cb, wdn, row(g_final), tm, final_norm=(l == depth - 1))
    return x
```

```python
import functools

import jax
import jax.numpy as jnp
from jax import lax
from jax.experimental import pallas as pl
from jax.experimental.pallas import tpu as pltpu

CHUNK = 64
N_ATT_HEADS = 8
ATT_HEAD_DIM = 64
KV_LATENT = 128
N_IDX_HEADS = 8
IDX_HEAD_DIM = 64
TOPK_MAX = 256
POOL_WINDOWS = (2, 4, 8, 16)
POOL_GROUP = 128
CONV_WIDTH = 3
EPS = 1e-6

COL_Q = N_ATT_HEADS * ATT_HEAD_DIM
COL_QI = N_IDX_HEADS * IDX_HEAD_DIM
POOL_WIDTH = POOL_GROUP * len(POOL_WINDOWS)
TAIL = 128
POOL_HALO = 16
CONV_HALO = 8
MASKED_DIST = 1e30
FIRST_BISECT_STEPS = 12
MORE_BISECT_STEPS = 3
TIE_BISECT_STEPS = 8
MAX_BISECT_ROUNDS = 100
ONES_ROWS = 16
SUBLANES = 8
KEY_TILE = 256
LOG2E = 1.4426950408889634
VMEM_LIMIT = 56 * 1024 * 1024

F32 = jnp.float32
BF16 = jnp.bfloat16


def _rms(x, g):
    return x * lax.rsqrt(jnp.mean(x * x, axis=-1, keepdims=True) + EPS) * g


def _proj_kernel(x_ref, g_ref, w_ref, wuk_ref, gkv_ref,
                 qlt_ref, c_ref, cxt_ref, qit_ref, kx_ref, wit_ref, u_ref):
    h = _rms(x_ref[...], g_ref[...]).astype(BF16)
    p = jnp.dot(h, w_ref[...], preferred_element_type=F32)
    tm = p.shape[0]
    o_c = COL_Q
    o_qi = o_c + KV_LATENT
    o_u = o_qi + COL_QI
    o_t = o_u + POOL_WIDTH
    q = p[:, :o_c].astype(BF16)
    for hd in range(N_ATT_HEADS):
        ql = jnp.dot(q[:, hd * ATT_HEAD_DIM:(hd + 1) * ATT_HEAD_DIM], wuk_ref[hd],
                     preferred_element_type=F32)
        qlt_ref[hd] = (ql * (ATT_HEAD_DIM ** -0.5 * LOG2E)).T.astype(BF16)
    c = _rms(p[:, o_c:o_qi], gkv_ref[...])
    c_ref[...] = c.astype(BF16)
    ones_rows = jnp.where(lax.broadcasted_iota(jnp.int32, (ONES_ROWS, tm), 0) == 0, 1.0, 0.0)
    cxt = jnp.concatenate([c.T, ones_rows], axis=0).astype(BF16)
    for tt in range(cxt_ref.shape[0]):
        cxt_ref[tt] = cxt[:, tt * KEY_TILE:(tt + 1) * KEY_TILE]
    qit_ref[...] = p[:, o_qi:o_u].T.astype(BF16)
    u_ref[...] = p[:, o_u:o_t]
    tail = p[:, o_t:o_t + TAIL]
    kx_ref[...] = tail[:, :IDX_HEAD_DIM].astype(BF16)
    wit_ref[...] = tail.T[IDX_HEAD_DIM:IDX_HEAD_DIM + N_IDX_HEADS] * (
        (N_IDX_HEADS ** -0.5) * (IDX_HEAD_DIM ** -0.5))


def _proj(x, g, w_cat, wuk, gkv, tm):
    B, S, D = x.shape
    ncol = w_cat.shape[1]
    grid = (B, S // tm)
    const = lambda *shape: pl.BlockSpec(shape, lambda b, i: (0,) * len(shape))
    return pl.pallas_call(
        _proj_kernel,
        grid=grid,
        in_specs=[
            pl.BlockSpec((None, tm, D), lambda b, i: (b, i, 0)),
            const(1, D),
            const(D, ncol),
            const(N_ATT_HEADS, ATT_HEAD_DIM, KV_LATENT),
            const(1, KV_LATENT),
        ],
        out_specs=[
            pl.BlockSpec((None, N_ATT_HEADS, KV_LATENT, tm), lambda b, i: (b, 0, 0, i)),
            pl.BlockSpec((None, tm, KV_LATENT), lambda b, i: (b, i, 0)),
            pl.BlockSpec((None, tm // KEY_TILE, KV_LATENT + ONES_ROWS, KEY_TILE), lambda b, i: (b, i, 0, 0)),
            pl.BlockSpec((None, COL_QI, tm), lambda b, i: (b, 0, i)),
            pl.BlockSpec((None, tm, IDX_HEAD_DIM), lambda b, i: (b, i, 0)),
            pl.BlockSpec((None, N_IDX_HEADS, tm), lambda b, i: (b, 0, i)),
            pl.BlockSpec((None, tm, POOL_WIDTH), lambda b, i: (b, i, 0)),
        ],
        out_shape=[
            jax.ShapeDtypeStruct((B, N_ATT_HEADS, KV_LATENT, S), BF16),
            jax.ShapeDtypeStruct((B, S, KV_LATENT), BF16),
            jax.ShapeDtypeStruct((B, S // KEY_TILE, KV_LATENT + ONES_ROWS, KEY_TILE), BF16),
            jax.ShapeDtypeStruct((B, COL_QI, S), BF16),
            jax.ShapeDtypeStruct((B, S, IDX_HEAD_DIM), BF16),
            jax.ShapeDtypeStruct((B, N_IDX_HEADS, S), F32),
            jax.ShapeDtypeStruct((B, S, POOL_WIDTH), F32),
        ],
        compiler_params=pltpu.CompilerParams(
            dimension_semantics=("parallel", "parallel"), vmem_limit_bytes=VMEM_LIMIT),
        name="proj",
    )(x, g, w_cat, wuk, gkv)


def _any(mask):
    return jnp.max(jnp.where(mask, 1.0, 0.0)) > 0.0


def _fold_rows(x, op):
    out = x[:SUBLANES]
    for g in range(1, x.shape[0] // SUBLANES):
        out = op(out, x[g * SUBLANES:(g + 1) * SUBLANES])
    return out


def _attn_kernel(qlt_ref, qit_ref, wit_ref, kx_ref, c_ref, cxt_ref, wuvt_ref, a_ref,
                 sm_ref, dm_ref, lg_ref, mx_ref, ox_ref, *, n_top):
    qb = qit_ref.shape[1]
    n_tiles, _, tk = cxt_ref.shape
    dc = c_ref.shape[1]
    j = pl.program_id(1)
    nk = (j + 1) * (qb // tk)

    qpos = j * qb + lax.broadcasted_iota(jnp.int32, (1, qb), 1)
    qchunk = qpos // CHUNK
    n_adm = ((qchunk + 1) * CHUNK).astype(F32)
    k_sel = jnp.minimum(n_adm, float(n_top))
    pos_inf = jnp.full((1, qb), jnp.inf, F32)
    group_zeros = jnp.zeros((SUBLANES, qb), F32)
    group_inf = jnp.full((SUBLANES, qb), jnp.inf, F32)

    def kpos_of(t):
        return t * tk + lax.broadcasted_iota(jnp.int32, (tk, 1), 0)

    def keys_of(ref, t):
        return ref[pl.ds(pl.multiple_of(t * tk, tk), tk), :]

    def col_sum(x):
        return jnp.sum(x, axis=0, keepdims=True)

    wit = wit_ref[...]

    def score_tile(t, carry):
        mn, mx = carry
        kx = keys_of(kx_ref, t)
        score = None
        for i in range(N_IDX_HEADS):
            rel = jnp.dot(kx, qit_ref[i * IDX_HEAD_DIM:(i + 1) * IDX_HEAD_DIM, :],
                          preferred_element_type=F32)
            term = wit[i:i + 1, :] * jnp.maximum(rel, 0.0)
            score = term if score is None else score + term
        adm = (kpos_of(t) // CHUNK) <= qchunk
        sm = jnp.where(adm, score, -jnp.inf)
        sm_ref[t] = sm
        c = keys_of(c_ref, t)
        for hd in range(N_ATT_HEADS):
            lg_ref[hd, t] = jnp.dot(c, qlt_ref[hd], preferred_element_type=F32)
        return (jnp.minimum(mn, _fold_rows(jnp.where(adm, score, jnp.inf), jnp.minimum)),
                jnp.maximum(mx, _fold_rows(sm, jnp.maximum)))

    mn, mx = lax.fori_loop(0, nk, score_tile, (group_inf, -group_inf))
    q_min = jnp.min(mn, axis=0, keepdims=True)
    q_max = jnp.max(mx, axis=0, keepdims=True)

    state = (q_min, q_max + (jnp.abs(q_max) * (2.0 ** -20) + 1e-30), n_adm, jnp.zeros_like(n_adm))

    def count_ge(mid):
        def body(t, acc):
            return acc + _fold_rows(jnp.where(sm_ref[t] >= mid, 1.0, 0.0), jnp.add)
        return col_sum(lax.fori_loop(0, nk, body, group_zeros))

    def bisect(n, st):
        lo, hi, clo, chi = st
        for _ in range(n):
            mid = lo + 0.5 * (hi - lo)
            cnt = count_ge(mid)
            ge = cnt >= k_sel
            lo, hi = jnp.where(ge, mid, lo), jnp.where(ge, hi, mid)
            clo, chi = jnp.where(ge, cnt, clo), jnp.where(ge, chi, cnt)
        return lo, hi, clo, chi

    def in_range_ends(st):
        lo, hi = st[0], st[1]

        def body(t, carry):
            s = sm_ref[t]
            return (jnp.minimum(carry[0], _fold_rows(jnp.where(s >= lo, s, jnp.inf), jnp.minimum)),
                    jnp.maximum(carry[1], _fold_rows(jnp.where(s < hi, s, -jnp.inf), jnp.maximum)))

        v_lo, v_hi = lax.fori_loop(0, nk, body, (group_inf, -group_inf))
        return jnp.min(v_lo, axis=0, keepdims=True), jnp.max(v_hi, axis=0, keepdims=True)

    def peel(st):
        lo, hi, clo, chi = st
        v_lo, v_hi = in_range_ends(st)
        done = clo == k_sel
        one_short = jnp.logical_and(jnp.logical_not(done), k_sel - chi == 1.0)
        one_over = jnp.logical_and(jnp.logical_not(jnp.logical_or(done, one_short)), clo - k_sel == 1.0)
        settled = jnp.logical_or(jnp.logical_or(done, one_short), jnp.logical_or(one_over, v_lo == v_hi))
        return (jnp.where(one_short, v_hi, lo), jnp.where(one_over, v_lo, pos_inf),
                _any(jnp.logical_not(settled)))

    state = bisect(FIRST_BISECT_STEPS, state)

    def refine(carry):
        _, rounds, st, _, _ = carry
        lo_sel, drop, is_open = peel(st)
        st = lax.cond(is_open, functools.partial(bisect, MORE_BISECT_STEPS), lambda s: s, st)
        return is_open.astype(jnp.int32), rounds + 1, st, lo_sel, drop

    undecided = _any(state[2] != k_sel)
    _, _, state, lo_sel, drop = lax.while_loop(
        lambda c: jnp.logical_and(c[0] > 0, c[1] < MAX_BISECT_ROUNDS), refine,
        (undecided.astype(jnp.int32), jnp.int32(0), state, state[0], pos_inf))

    def dist_of(t):
        return jnp.abs(qpos - kpos_of(t)).astype(F32)

    def select_tile(t, acc):
        s = sm_ref[t]
        sel = jnp.logical_and(s >= lo_sel, s != drop)
        dm_ref[t] = jnp.where(sel, dist_of(t), MASKED_DIST)
        return acc + _fold_rows(jnp.where(sel, 1.0, 0.0), jnp.add)

    n_sel = col_sum(lax.fori_loop(0, nk, select_tile, group_zeros))

    @pl.when(_any(n_sel != k_sel))
    def _():
        def unresolved(st):
            v_lo, v_hi = in_range_ends(st)
            return _any(jnp.logical_and(st[2] != k_sel, v_lo != v_hi)).astype(jnp.int32)

        def tighten(carry):
            _, rounds, st = carry
            st = bisect(TIE_BISECT_STEPS, st)
            return unresolved(st), rounds + 1, st

        _, _, (lo, hi, _, chi) = lax.while_loop(
            lambda c: jnp.logical_and(c[0] > 0, c[1] < MAX_BISECT_ROUNDS), tighten,
            (unresolved(state), jnp.int32(0), state))
        need = k_sel - chi

        def in_range_upto(t, bound):
            s = sm_ref[t]
            return jnp.logical_and(jnp.logical_and(s >= lo, s < hi), kpos_of(t).astype(F32) <= bound)

        def tie_step(_, bounds):
            lo_i, hi_i = bounds
            mid_i = jnp.floor(0.5 * (lo_i + hi_i))

            def body(t, acc):
                return acc + _fold_rows(jnp.where(in_range_upto(t, mid_i), 1.0, 0.0), jnp.add)

            ok = col_sum(lax.fori_loop(0, nk, body, group_zeros)) >= need
            return jnp.where(ok, lo_i, mid_i), jnp.where(ok, mid_i, hi_i)

        n_steps = max(1, (n_tiles * tk - 1).bit_length()) + 1
        init = (jnp.full((1, qb), -1.0, F32), jnp.zeros((1, qb), F32) + (nk * tk - 1).astype(F32))
        cut = lax.fori_loop(0, n_steps, tie_step, init)[1]

        def reselect_tile(t, _):
            sel_x = jnp.logical_or(sm_ref[t] >= hi, in_range_upto(t, cut))
            dm_ref[t] = jnp.where(sel_x, dist_of(t), MASKED_DIST)
            return 0

        lax.fori_loop(0, nk, reselect_tile, 0)

    slopes = [LOG2E * 2.0 ** (-8.0 * (hd + 1) / N_ATT_HEADS) for hd in range(N_ATT_HEADS)]
    for hd in range(N_ATT_HEADS):
        mx_ref[hd] = -group_inf
        ox_ref[hd] = jnp.zeros(ox_ref.shape[1:], F32)

    def top_tile(t, _):
        dm = dm_ref[t]
        for hd in range(N_ATT_HEADS):
            mx_ref[hd] = jnp.maximum(mx_ref[hd], _fold_rows(lg_ref[hd, t] - slopes[hd] * dm, jnp.maximum))
        return 0

    lax.fori_loop(0, nk, top_tile, 0)
    q_top = [jnp.max(mx_ref[hd], axis=0, keepdims=True) for hd in range(N_ATT_HEADS)]

    def pv_tile(t, _):
        dm = dm_ref[t]
        cxt = cxt_ref[t]
        for hd in range(N_ATT_HEADS):
            p = jnp.exp2(lg_ref[hd, t] - slopes[hd] * dm - q_top[hd]).astype(BF16)
            ox_ref[hd] += jnp.dot(cxt, p, preferred_element_type=F32)
        return 0

    lax.fori_loop(0, nk, pv_tile, 0)
    a_heads = []
    for hd in range(N_ATT_HEADS):
        ox = ox_ref[hd]
        o_t = (ox[:dc] / ox[dc:dc + 1]).astype(BF16)
        a_heads.append(jnp.dot(wuvt_ref[hd], o_t, preferred_element_type=F32))
    a_ref[...] = jnp.concatenate(a_heads, axis=0).T.astype(BF16)


def _attn(qlt, qit, wit, kx, c, cxt, wuvt, qb, n_top):
    B, H, dc, S = qlt.shape
    n_tiles, ox_rows, tk = cxt.shape[1:]
    grid = (B, S // qb)
    return pl.pallas_call(
        functools.partial(_attn_kernel, n_top=n_top),
        grid=grid,
        in_specs=[
            pl.BlockSpec((None, H, dc, qb), lambda b, j: (b, 0, 0, j)),
            pl.BlockSpec((None, COL_QI, qb), lambda b, j: (b, 0, j)),
            pl.BlockSpec((None, N_IDX_HEADS, qb), lambda b, j: (b, 0, j)),
            pl.BlockSpec((None, S, IDX_HEAD_DIM), lambda b, j: (b, 0, 0)),
            pl.BlockSpec((None, S, dc), lambda b, j: (b, 0, 0)),
            pl.BlockSpec((None, n_tiles, ox_rows, tk), lambda b, j: (b, 0, 0, 0)),
            pl.BlockSpec((H, ATT_HEAD_DIM, dc), lambda b, j: (0, 0, 0)),
        ],
        out_specs=pl.BlockSpec((None, qb, COL_Q), lambda b, j: (b, j, 0)),
        out_shape=jax.ShapeDtypeStruct((B, S, COL_Q), BF16),
        scratch_shapes=[
            pltpu.VMEM((n_tiles, tk, qb), F32),
            pltpu.VMEM((n_tiles, tk, qb), F32),
            pltpu.VMEM((H, n_tiles, tk, qb), F32),
            pltpu.VMEM((H, SUBLANES, qb), F32),
            pltpu.VMEM((H, ox_rows, qb), F32),
        ],
        compiler_params=pltpu.CompilerParams(
            dimension_semantics=("parallel", "parallel"), vmem_limit_bytes=VMEM_LIMIT),
        name="attn",
    )(qlt, qit, wit, kx, c, cxt, wuvt)


def _mix_kernel(x_ref, a_ref, u_ref, uh_ref, wp_ref, ps_ref, wo_ref, o_ref):
    ts = u_ref.shape[0]
    i = pl.program_id(1)
    halo = jnp.where(i == 0, 0.0, uh_ref[...])
    ext = jnp.concatenate([halo, u_ref[...]], axis=0)
    t = i * ts + lax.broadcasted_iota(jnp.int32, (ts, 1), 0)
    mixed = []
    for g, win in enumerate(POOL_WINDOWS):
        e = ext[:, g * POOL_GROUP:(g + 1) * POOL_GROUP]
        acc = e
        span = 1
        while span < win:
            acc = acc + pltpu.roll(acc, span, axis=0)
            span *= 2
        count = jnp.minimum(t + 1, win).astype(F32)
        pooled = acc[POOL_HALO:] / count - e[POOL_HALO:]
        mixed.append(jnp.dot(pooled.astype(BF16), wp_ref[g], preferred_element_type=F32))
    b = (jnp.concatenate(mixed, axis=1) * ps_ref[...]).astype(BF16)
    ab = jnp.concatenate([a_ref[...], b], axis=1)
    o_ref[...] = x_ref[...] + jnp.dot(ab, wo_ref[...], preferred_element_type=F32)


def _mix(x, a, u, w_pool, pool_scale, w_o, ts):
    B, S, D = x.shape
    grid = (B, S // ts)
    hb = ts // POOL_HALO
    return pl.pallas_call(
        _mix_kernel,
        grid=grid,
        in_specs=[
            pl.BlockSpec((None, ts, D), lambda b, i: (b, i, 0)),
            pl.BlockSpec((None, ts, COL_Q), lambda b, i: (b, i, 0)),
            pl.BlockSpec((None, ts, POOL_WIDTH), lambda b, i: (b, i, 0)),
            pl.BlockSpec((None, POOL_HALO, POOL_WIDTH), lambda b, i: (b, jnp.maximum(i * hb - 1, 0), 0)),
            pl.BlockSpec((len(POOL_WINDOWS), POOL_GROUP, POOL_GROUP), lambda b, i: (0, 0, 0)),
            pl.BlockSpec((1, POOL_WIDTH), lambda b, i: (0, 0)),
            pl.BlockSpec((COL_Q + POOL_WIDTH, D), lambda b, i: (0, 0)),
        ],
        out_specs=pl.BlockSpec((None, ts, D), lambda b, i: (b, i, 0)),
        out_shape=jax.ShapeDtypeStruct((B, S, D), F32),
        compiler_params=pltpu.CompilerParams(
            dimension_semantics=("parallel", "parallel"), vmem_limit_bytes=VMEM_LIMIT),
        name="mix",
    )(x, a, u, u, w_pool, pool_scale, w_o)


def _ffn_kernel(x_ref, xh_ref, g_ref, wup_ref, cw_ref, cb_ref, wdn_ref, gf_ref, o_ref, act_ref, *, final_norm):
    tm, D = x_ref.shape
    n_chunks, _, fc2 = wup_ref.shape
    fc = fc2 // 2
    i = pl.program_id(1)
    x = x_ref[...]
    xe = jnp.concatenate([xh_ref[...], x], axis=0)
    h = _rms(xe, g_ref[...])
    row = lax.broadcasted_iota(jnp.int32, (CONV_HALO + tm, 1), 0)
    h = jnp.where(jnp.logical_and(i == 0, row < CONV_HALO), 0.0, h).astype(BF16)

    for k in range(n_chunks):
        a = jnp.dot(h, wup_ref[k], preferred_element_type=F32)
        cw = cw_ref[k]
        conv = cb_ref[k] + cw[2:3] * a
        for jj in range(CONV_WIDTH - 1):
            conv = conv + cw[jj:jj + 1] * pltpu.roll(a, CONV_WIDTH - 1 - jj, axis=0)
        conv = conv[CONV_HALO:]
        gate = conv[:, :fc]
        act = gate * (1.0 / (1.0 + jnp.exp(-gate))) * conv[:, fc:]
        act_ref[:, k * fc:(k + 1) * fc] = act.astype(BF16)

    y = x + jnp.dot(act_ref[...], wdn_ref[...], preferred_element_type=F32)
    if final_norm:
        y = _rms(y, gf_ref[...])
    o_ref[...] = y


def _ffn(x, g, wup, cw, cb, wdn, g_final, tm, final_norm):
    B, S, D = x.shape
    grid = (B, S // tm)
    hb = tm // CONV_HALO
    const = lambda *shape: pl.BlockSpec(shape, lambda b, i: (0,) * len(shape))
    return pl.pallas_call(
        functools.partial(_ffn_kernel, final_norm=final_norm),
        grid=grid,
        in_specs=[
            pl.BlockSpec((None, tm, D), lambda b, i: (b, i, 0)),
            pl.BlockSpec((None, CONV_HALO, D), lambda b, i: (b, jnp.maximum(i * hb - 1, 0), 0)),
            const(1, D),
            const(*wup.shape),
            const(*cw.shape),
            const(*cb.shape),
            const(*wdn.shape),
            const(1, D),
        ],
        out_specs=pl.BlockSpec((None, tm, D), lambda b, i: (b, i, 0)),
        out_shape=jax.ShapeDtypeStruct((B, S, D), F32),
        scratch_shapes=[pltpu.VMEM((tm, wdn.shape[0]), BF16)],
        compiler_params=pltpu.CompilerParams(
            dimension_semantics=("parallel", "parallel"), vmem_limit_bytes=VMEM_LIMIT),
        name="ffn",
    )(x, x, g, wup, cw, cb, wdn, g_final)


def _ffn_chunk(d_ff):
    for fc in (256, 128):
        if d_ff % fc == 0:
            return fc
    raise ValueError(f"unsupported FFN width {d_ff}")


def kernel(x, g_mix, w_in, g_kv, w_uk, w_uv, w_pool, pool_scale, w_o, g_ffn, w_up, conv_w, conv_b, w_down, g_final):
    B, S, D = x.shape
    depth = g_mix.shape[0]
    d_ff = w_down.shape[1]
    fc = _ffn_chunk(d_ff)
    n_chunks = d_ff // fc
    n_top = min(TOPK_MAX, S // 4)
    tm = min(512, S)
    qb = min(256, S)
    assert S % tm == 0 and S % qb == 0 and qb % KEY_TILE == 0 and tm % KEY_TILE == 0 and KEY_TILE % CHUNK == 0
    assert w_in.shape[2] == COL_Q + KV_LATENT + COL_QI + IDX_HEAD_DIM + N_IDX_HEADS + POOL_WIDTH

    s_q = COL_Q
    s_c = s_q + KV_LATENT
    s_qi = s_c + COL_QI
    s_ki = s_qi + IDX_HEAD_DIM
    s_wi = s_ki + N_IDX_HEADS
    row = lambda v: v.reshape(1, -1).astype(F32)

    for l in range(depth):
        w = w_in[l]
        w_cat = jnp.concatenate(
            [w[:, :s_q], w[:, s_q:s_c], w[:, s_c:s_qi], w[:, s_wi:], w[:, s_qi:s_ki], w[:, s_ki:s_wi],
             jnp.zeros((D, TAIL - IDX_HEAD_DIM - N_IDX_HEADS), w.dtype)], axis=1).astype(BF16)
        wuvt = jnp.swapaxes(w_uv[l], 1, 2).astype(BF16)
        pair = lambda m: jnp.concatenate(
            [m[..., :d_ff].reshape(m.shape[:-1] + (n_chunks, fc)),
             m[..., d_ff:].reshape(m.shape[:-1] + (n_chunks, fc))], axis=-1)
        wup = jnp.moveaxis(pair(w_up[l]), 1, 0).astype(BF16)
        cw = jnp.moveaxis(pair(conv_w[l]), 1, 0).astype(F32)
        cb = pair(conv_b[l]).reshape(n_chunks, 1, 2 * fc).astype(F32)
        wdn = w_down[l].astype(BF16)

        qlt, c, cxt, qit, kx, wit, u = _proj(x, row(g_mix[l]), w_cat, w_uk[l].astype(BF16), row(g_kv[l]), tm)
        a = _attn(qlt, qit, wit, kx, c, cxt, wuvt, qb, n_top)
        x = _mix(x, a, u, w_pool[l].astype(BF16), row(pool_scale[l]), w_o[l].astype(BF16), tm)
        x = _ffn(x, row(g_ffn[l]), wup, cw, cb, wdn, row(g_final), tm, final_norm=(l == depth - 1))
    return x
```

```python
import functools

import jax
import jax.numpy as jnp
from jax import lax
from jax.experimental import pallas as pl
from jax.experimental.pallas import tpu as pltpu

CHUNK = 64
N_ATT_HEADS = 8
ATT_HEAD_DIM = 64
KV_LATENT = 128
N_IDX_HEADS = 8
IDX_HEAD_DIM = 64
TOPK_MAX = 256
POOL_WINDOWS = (2, 4, 8, 16)
POOL_GROUP = 128
CONV_WIDTH = 3
EPS = 1e-6

COL_Q = N_ATT_HEADS * ATT_HEAD_DIM
COL_QI = N_IDX_HEADS * IDX_HEAD_DIM
POOL_WIDTH = POOL_GROUP * len(POOL_WINDOWS)
TAIL = 128
POOL_HALO = 16
CONV_HALO = 8
MASKED_DIST = 1e30
FIRST_BISECT_STEPS = 14
MORE_BISECT_STEPS = 2
TIE_BISECT_STEPS = 8
MAX_BISECT_ROUNDS = 100
ONES_ROWS = 16
SUBLANES = 8
FOLD_WAYS = 4
KEY_TILE = 256
LOG2E = 1.4426950408889634
VMEM_LIMIT = 56 * 1024 * 1024

F32 = jnp.float32
BF16 = jnp.bfloat16


def _rms(x, g):
    return x * lax.rsqrt(jnp.mean(x * x, axis=-1, keepdims=True) + EPS) * g


def _proj_kernel(x_ref, g_ref, w_ref, wuk_ref, gkv_ref,
                 qlt_ref, c_ref, cxt_ref, qit_ref, kx_ref, wit_ref, u_ref):
    h = _rms(x_ref[...], g_ref[...]).astype(BF16)
    p = jnp.dot(h, w_ref[...], preferred_element_type=F32)
    tm = p.shape[0]
    o_c = COL_Q
    o_qi = o_c + KV_LATENT
    o_u = o_qi + COL_QI
    o_t = o_u + POOL_WIDTH
    q = p[:, :o_c].astype(BF16)
    for hd in range(N_ATT_HEADS):
        ql = jnp.dot(q[:, hd * ATT_HEAD_DIM:(hd + 1) * ATT_HEAD_DIM], wuk_ref[hd],
                     preferred_element_type=F32)
        qlt_ref[hd] = (ql * (ATT_HEAD_DIM ** -0.5 * LOG2E)).T.astype(BF16)
    c = _rms(p[:, o_c:o_qi], gkv_ref[...])
    c_ref[...] = c.astype(BF16)
    ones_rows = jnp.where(lax.broadcasted_iota(jnp.int32, (ONES_ROWS, tm), 0) == 0, 1.0, 0.0)
    cxt = jnp.concatenate([c.T, ones_rows], axis=0).astype(BF16)
    for tt in range(cxt_ref.shape[0]):
        cxt_ref[tt] = cxt[:, tt * KEY_TILE:(tt + 1) * KEY_TILE]
    qit_ref[...] = p[:, o_qi:o_u].T.astype(BF16)
    u_ref[...] = p[:, o_u:o_t]
    tail = p[:, o_t:o_t + TAIL]
    kx_ref[...] = tail[:, :IDX_HEAD_DIM].astype(BF16)
    wit_ref[...] = tail.T[IDX_HEAD_DIM:IDX_HEAD_DIM + N_IDX_HEADS] * (
        (N_IDX_HEADS ** -0.5) * (IDX_HEAD_DIM ** -0.5))


def _proj(x, g, w_cat, wuk, gkv, tm):
    B, S, D = x.shape
    ncol = w_cat.shape[1]
    grid = (B, S // tm)
    const = lambda *shape: pl.BlockSpec(shape, lambda b, i: (0,) * len(shape))
    return pl.pallas_call(
        _proj_kernel,
        grid=grid,
        in_specs=[
            pl.BlockSpec((None, tm, D), lambda b, i: (b, i, 0)),
            const(1, D),
            const(D, ncol),
            const(N_ATT_HEADS, ATT_HEAD_DIM, KV_LATENT),
            const(1, KV_LATENT),
        ],
        out_specs=[
            pl.BlockSpec((None, N_ATT_HEADS, KV_LATENT, tm), lambda b, i: (b, 0, 0, i)),
            pl.BlockSpec((None, tm, KV_LATENT), lambda b, i: (b, i, 0)),
            pl.BlockSpec((None, tm // KEY_TILE, KV_LATENT + ONES_ROWS, KEY_TILE), lambda b, i: (b, i, 0, 0)),
            pl.BlockSpec((None, COL_QI, tm), lambda b, i: (b, 0, i)),
            pl.BlockSpec((None, tm, IDX_HEAD_DIM), lambda b, i: (b, i, 0)),
            pl.BlockSpec((None, N_IDX_HEADS, tm), lambda b, i: (b, 0, i)),
            pl.BlockSpec((None, tm, POOL_WIDTH), lambda b, i: (b, i, 0)),
        ],
        out_shape=[
            jax.ShapeDtypeStruct((B, N_ATT_HEADS, KV_LATENT, S), BF16),
            jax.ShapeDtypeStruct((B, S, KV_LATENT), BF16),
            jax.ShapeDtypeStruct((B, S // KEY_TILE, KV_LATENT + ONES_ROWS, KEY_TILE), BF16),
            jax.ShapeDtypeStruct((B, COL_QI, S), BF16),
            jax.ShapeDtypeStruct((B, S, IDX_HEAD_DIM), BF16),
            jax.ShapeDtypeStruct((B, N_IDX_HEADS, S), F32),
            jax.ShapeDtypeStruct((B, S, POOL_WIDTH), F32),
        ],
        compiler_params=pltpu.CompilerParams(
            dimension_semantics=("parallel", "parallel"), vmem_limit_bytes=VMEM_LIMIT),
        name="proj",
    )(x, g, w_cat, wuk, gkv)


def _any(mask):
    return jnp.max(jnp.where(mask, 1.0, 0.0)) > 0.0


def _fold_rows(x, op, ways=FOLD_WAYS):
    groups = [x[g * SUBLANES:(g + 1) * SUBLANES] for g in range(x.shape[0] // SUBLANES)]
    parts = groups[:ways]
    for g, grp in enumerate(groups[ways:]):
        parts[g % ways] = op(parts[g % ways], grp)
    while len(parts) > 1:
        parts = [op(parts[i], parts[i + 1]) if i + 1 < len(parts) else parts[i] for i in range(0, len(parts), 2)]
    return parts[0]


def _attn_kernel(qlt_ref, qit_ref, wit_ref, kx_ref, c_ref, cxt_ref, wuvt_ref, a_ref,
                 sm_ref, dm_ref, lg_ref, mx_ref, ox_ref, *, n_top):
    qb = qit_ref.shape[1]
    n_tiles, _, tk = cxt_ref.shape
    dc = c_ref.shape[1]
    j = pl.program_id(1)
    nk = (j + 1) * (qb // tk)

    qpos = j * qb + lax.broadcasted_iota(jnp.int32, (1, qb), 1)
    qchunk = qpos // CHUNK
    n_adm = ((qchunk + 1) * CHUNK).astype(F32)
    k_sel = jnp.minimum(n_adm, float(n_top))
    pos_inf = jnp.full((1, qb), jnp.inf, F32)
    group_zeros = jnp.zeros((SUBLANES, qb), F32)
    group_inf = jnp.full((SUBLANES, qb), jnp.inf, F32)

    def kpos_of(t, n=tk):
        return t * n + lax.broadcasted_iota(jnp.int32, (n, 1), 0)

    n_pairs = (nk + 1) // 2

    def half_of(t):
        return pl.ds(pl.multiple_of((t % 2) * tk, tk), tk)

    @pl.when(nk % 2 == 1)
    def _():
        sm_ref[nk // 2, tk:, :] = jnp.full((tk, qb), -jnp.inf, F32)

    def keys_of(ref, t):
        return ref[pl.ds(pl.multiple_of(t * tk, tk), tk), :]

    def col_sum(x):
        return jnp.sum(x, axis=0, keepdims=True)

    wit = wit_ref[...]

    def score_tile(t, carry):
        mn, mx = carry
        kx = keys_of(kx_ref, t)
        score = None
        for i in range(N_IDX_HEADS):
            rel = jnp.dot(kx, qit_ref[i * IDX_HEAD_DIM:(i + 1) * IDX_HEAD_DIM, :],
                          preferred_element_type=F32)
            term = wit[i:i + 1, :] * jnp.maximum(rel, 0.0)
            score = term if score is None else score + term
        adm = (kpos_of(t) // CHUNK) <= qchunk
        sm = jnp.where(adm, score, -jnp.inf)
        sm_ref[t // 2, half_of(t), :] = sm
        c = keys_of(c_ref, t)
        for hd in range(N_ATT_HEADS):
            lg_ref[hd, t] = jnp.dot(c, qlt_ref[hd], preferred_element_type=F32)
        return (jnp.minimum(mn, _fold_rows(jnp.where(adm, score, jnp.inf), jnp.minimum)),
                jnp.maximum(mx, _fold_rows(sm, jnp.maximum)))

    mn, mx = lax.fori_loop(0, nk, score_tile, (group_inf, -group_inf))
    q_min = jnp.min(mn, axis=0, keepdims=True)
    q_max = jnp.max(mx, axis=0, keepdims=True)

    state = (q_min, q_max + (jnp.abs(q_max) * (2.0 ** -20) + 1e-30), n_adm, jnp.zeros_like(n_adm))

    def count_ge(mid):
        def body(pr, acc):
            return acc + _fold_rows(jnp.where(sm_ref[pr] >= mid, 1.0, 0.0), jnp.add)
        return col_sum(lax.fori_loop(0, n_pairs, body, group_zeros))

    def bisect(n, st):
        lo, hi, clo, chi = st
        for _ in range(n):
            mid = lo + 0.5 * (hi - lo)
            cnt = count_ge(mid)
            ge = cnt >= k_sel
            lo, hi = jnp.where(ge, mid, lo), jnp.where(ge, hi, mid)
            clo, chi = jnp.where(ge, cnt, clo), jnp.where(ge, chi, cnt)
        return lo, hi, clo, chi

    def in_range_ends(st):
        lo, hi = st[0], st[1]

        def body(pr, carry):
            s = sm_ref[pr]
            return (jnp.minimum(carry[0], _fold_rows(jnp.where(s >= lo, s, jnp.inf), jnp.minimum)),
                    jnp.maximum(carry[1], _fold_rows(jnp.where(s < hi, s, -jnp.inf), jnp.maximum)))

        v_lo, v_hi = lax.fori_loop(0, n_pairs, body, (group_inf, -group_inf))
        return jnp.min(v_lo, axis=0, keepdims=True), jnp.max(v_hi, axis=0, keepdims=True)

    def peel(st):
        lo, hi, clo, chi = st
        v_lo, v_hi = in_range_ends(st)
        done = clo == k_sel
        one_short = jnp.logical_and(jnp.logical_not(done), k_sel - chi == 1.0)
        one_over = jnp.logical_and(jnp.logical_not(jnp.logical_or(done, one_short)), clo - k_sel == 1.0)
        settled = jnp.logical_or(jnp.logical_or(done, one_short), jnp.logical_or(one_over, v_lo == v_hi))
        return (jnp.where(one_short, v_hi, lo), jnp.where(one_over, v_lo, pos_inf),
                _any(jnp.logical_not(settled)))

    state = bisect(FIRST_BISECT_STEPS, state)

    def refine(carry):
        _, rounds, st, _, _ = carry
        lo_sel, drop, is_open = peel(st)
        st = lax.cond(is_open, functools.partial(bisect, MORE_BISECT_STEPS), lambda s: s, st)
        return is_open.astype(jnp.int32), rounds + 1, st, lo_sel, drop

    undecided = _any(state[2] != k_sel)
    _, _, state, lo_sel, drop = lax.while_loop(
        lambda c: jnp.logical_and(c[0] > 0, c[1] < MAX_BISECT_ROUNDS), refine,
        (undecided.astype(jnp.int32), jnp.int32(0), state, state[0], pos_inf))

    def select_pair(pr, acc):
        s = sm_ref[pr]
        sel = jnp.logical_and(s >= lo_sel, s != drop)
        dist = jnp.abs(qpos - kpos_of(pr, 2 * tk)).astype(F32)
        dm_ref[pr] = jnp.where(sel, dist, MASKED_DIST)
        return acc + _fold_rows(jnp.where(sel, 1.0, 0.0), jnp.add)

    n_sel = col_sum(lax.fori_loop(0, n_pairs, select_pair, group_zeros))

    @pl.when(_any(n_sel != k_sel))
    def _():
        def unresolved(st):
            v_lo, v_hi = in_range_ends(st)
            return _any(jnp.logical_and(st[2] != k_sel, v_lo != v_hi)).astype(jnp.int32)

        def tighten(carry):
            _, rounds, st = carry
            st = bisect(TIE_BISECT_STEPS, st)
            return unresolved(st), rounds + 1, st

        _, _, (lo, hi, _, chi) = lax.while_loop(
            lambda c: jnp.logical_and(c[0] > 0, c[1] < MAX_BISECT_ROUNDS), tighten,
            (unresolved(state), jnp.int32(0), state))
        need = k_sel - chi
        key_i = lax.broadcasted_iota(jnp.int32, (tk, tk), 0)
        tri = jnp.where(key_i >= lax.broadcasted_iota(jnp.int32, (tk, tk), 1), 1.0, 0.0).astype(BF16)

        def reselect_tile(t, seen):
            s = sm_ref[t // 2, half_of(t), :]
            in_range = jnp.logical_and(s >= lo, s < hi)
            ones = jnp.where(in_range, 1.0, 0.0)
            rank = seen + jnp.dot(tri, ones.astype(BF16), preferred_element_type=F32)
            sel_x = jnp.logical_or(s >= hi, jnp.logical_and(in_range, rank <= need))
            dm_ref[t // 2, half_of(t), :] = jnp.where(sel_x, jnp.abs(qpos - kpos_of(t)).astype(F32), MASKED_DIST)
            return seen + col_sum(_fold_rows(ones, jnp.add))

        lax.fori_loop(0, nk, reselect_tile, jnp.zeros((1, qb), F32))

    slopes = [LOG2E * 2.0 ** (-8.0 * (hd + 1) / N_ATT_HEADS) for hd in range(N_ATT_HEADS)]
    for hd in range(N_ATT_HEADS):
        mx_ref[hd] = -group_inf
        ox_ref[hd] = jnp.zeros(ox_ref.shape[1:], F32)

    def top_tile(t, _):
        dm = dm_ref[t // 2, half_of(t), :]
        for hd in range(N_ATT_HEADS):
            logits = lg_ref[hd, t] - slopes[hd] * dm
            lg_ref[hd, t] = logits
            mx_ref[hd] = jnp.maximum(mx_ref[hd], _fold_rows(logits, jnp.maximum, ways=1))
        return 0

    lax.fori_loop(0, nk, top_tile, 0)
    q_top = [jnp.max(mx_ref[hd], axis=0, keepdims=True) for hd in range(N_ATT_HEADS)]

    def pv_tile(t, _):
        cxt = cxt_ref[t]
        for hd in range(N_ATT_HEADS):
            p = jnp.exp2(lg_ref[hd, t] - q_top[hd]).astype(BF16)
            ox_ref[hd] += jnp.dot(cxt, p, preferred_element_type=F32)
        return 0

    lax.fori_loop(0, nk, pv_tile, 0)
    a_heads = []
    for hd in range(N_ATT_HEADS):
        ox = ox_ref[hd]
        o_t = (ox[:dc] / ox[dc:dc + 1]).astype(BF16)
        a_heads.append(jnp.dot(wuvt_ref[hd], o_t, preferred_element_type=F32))
    a_ref[...] = jnp.concatenate(a_heads, axis=0).T.astype(BF16)


def _attn(qlt, qit, wit, kx, c, cxt, wuvt, qb, n_top):
    B, H, dc, S = qlt.shape
    n_tiles, ox_rows, tk = cxt.shape[1:]
    grid = (B, S // qb)
    return pl.pallas_call(
        functools.partial(_attn_kernel, n_top=n_top),
        grid=grid,
        in_specs=[
            pl.BlockSpec((None, H, dc, qb), lambda b, j: (b, 0, 0, j)),
            pl.BlockSpec((None, COL_QI, qb), lambda b, j: (b, 0, j)),
            pl.BlockSpec((None, N_IDX_HEADS, qb), lambda b, j: (b, 0, j)),
            pl.BlockSpec((None, S, IDX_HEAD_DIM), lambda b, j: (b, 0, 0)),
            pl.BlockSpec((None, S, dc), lambda b, j: (b, 0, 0)),
            pl.BlockSpec((None, n_tiles, ox_rows, tk), lambda b, j: (b, 0, 0, 0)),
            pl.BlockSpec((H, ATT_HEAD_DIM, dc), lambda b, j: (0, 0, 0)),
        ],
        out_specs=pl.BlockSpec((None, qb, COL_Q), lambda b, j: (b, j, 0)),
        out_shape=jax.ShapeDtypeStruct((B, S, COL_Q), BF16),
        scratch_shapes=[
            pltpu.VMEM(((n_tiles + 1) // 2, 2 * tk, qb), F32),
            pltpu.VMEM(((n_tiles + 1) // 2, 2 * tk, qb), F32),
            pltpu.VMEM((H, n_tiles, tk, qb), F32),
            pltpu.VMEM((H, SUBLANES, qb), F32),
            pltpu.VMEM((H, ox_rows, qb), F32),
        ],
        compiler_params=pltpu.CompilerParams(
            dimension_semantics=("parallel", "parallel"), vmem_limit_bytes=VMEM_LIMIT),
        name="attn",
    )(qlt, qit, wit, kx, c, cxt, wuvt)


def _mix_kernel(x_ref, a_ref, u_ref, uh_ref, wp_ref, ps_ref, wo_ref, o_ref):
    ts = u_ref.shape[0]
    i = pl.program_id(1)
    halo = jnp.where(i == 0, 0.0, uh_ref[...])
    ext = jnp.concatenate([halo, u_ref[...]], axis=0)
    t = i * ts + lax.broadcasted_iota(jnp.int32, (ts, 1), 0)
    mixed = []
    for g, win in enumerate(POOL_WINDOWS):
        e = ext[:, g * POOL_GROUP:(g + 1) * POOL_GROUP]
        acc = e
        span = 1
        while span < win:
            acc = acc + pltpu.roll(acc, span, axis=0)
            span *= 2
        count = jnp.minimum(t + 1, win).astype(F32)
        pooled = acc[POOL_HALO:] / count - e[POOL_HALO:]
        mixed.append(jnp.dot(pooled.astype(BF16), wp_ref[g], preferred_element_type=F32))
    b = (jnp.concatenate(mixed, axis=1) * ps_ref[...]).astype(BF16)
    ab = jnp.concatenate([a_ref[...], b], axis=1)
    o_ref[...] = x_ref[...] + jnp.dot(ab, wo_ref[...], preferred_element_type=F32)


def _mix(x, a, u, w_pool, pool_scale, w_o, ts):
    B, S, D = x.shape
    grid = (B, S // ts)
    hb = ts // POOL_HALO
    return pl.pallas_call(
        _mix_kernel,
        grid=grid,
        in_specs=[
            pl.BlockSpec((None, ts, D), lambda b, i: (b, i, 0)),
            pl.BlockSpec((None, ts, COL_Q), lambda b, i: (b, i, 0)),
            pl.BlockSpec((None, ts, POOL_WIDTH), lambda b, i: (b, i, 0)),
            pl.BlockSpec((None, POOL_HALO, POOL_WIDTH), lambda b, i: (b, jnp.maximum(i * hb - 1, 0), 0)),
            pl.BlockSpec((len(POOL_WINDOWS), POOL_GROUP, POOL_GROUP), lambda b, i: (0, 0, 0)),
            pl.BlockSpec((1, POOL_WIDTH), lambda b, i: (0, 0)),
            pl.BlockSpec((COL_Q + POOL_WIDTH, D), lambda b, i: (0, 0)),
        ],
        out_specs=pl.BlockSpec((None, ts, D), lambda b, i: (b, i, 0)),
        out_shape=jax.ShapeDtypeStruct((B, S, D), F32),
        compiler_params=pltpu.CompilerParams(
            dimension_semantics=("parallel", "parallel"), vmem_limit_bytes=VMEM_LIMIT),
        name="mix",
    )(x, a, u, u, w_pool, pool_scale, w_o)


def _ffn_kernel(x_ref, xh_ref, g_ref, wup_ref, cw_ref, cb_ref, wdn_ref, gf_ref, o_ref, act_ref, *, final_norm):
    tm, D = x_ref.shape
    n_chunks, _, fc2 = wup_ref.shape
    fc = fc2 // 2
    i = pl.program_id(1)
    x = x_ref[...]
    xe = jnp.concatenate([xh_ref[...], x], axis=0)
    h = _rms(xe, g_ref[...])
    row = lax.broadcasted_iota(jnp.int32, (CONV_HALO + tm, 1), 0)
    h = jnp.where(jnp.logical_and(i == 0, row < CONV_HALO), 0.0, h).astype(BF16)

    for k in range(n_chunks):
        a = jnp.dot(h, wup_ref[k], preferred_element_type=F32)
        cw = cw_ref[k]
        conv = cb_ref[k] + cw[2:3] * a
        for jj in range(CONV_WIDTH - 1):
            conv = conv + cw[jj:jj + 1] * pltpu.roll(a, CONV_WIDTH - 1 - jj, axis=0)
        conv = conv[CONV_HALO:]
        gate = conv[:, :fc]
        act = gate * (1.0 / (1.0 + jnp.exp(-gate))) * conv[:, fc:]
        act_ref[:, k * fc:(k + 1) * fc] = act.astype(BF16)

    y = x + jnp.dot(act_ref[...], wdn_ref[...], preferred_element_type=F32)
    if final_norm:
        y = _rms(y, gf_ref[...])
    o_ref[...] = y


def _ffn(x, g, wup, cw, cb, wdn, g_final, tm, final_norm):
    B, S, D = x.shape
    grid = (B, S // tm)
    hb = tm // CONV_HALO
    const = lambda *shape: pl.BlockSpec(shape, lambda b, i: (0,) * len(shape))
    return pl.pallas_call(
        functools.partial(_ffn_kernel, final_norm=final_norm),
        grid=grid,
        in_specs=[
            pl.BlockSpec((None, tm, D), lambda b, i: (b, i, 0)),
            pl.BlockSpec((None, CONV_HALO, D), lambda b, i: (b, jnp.maximum(i * hb - 1, 0), 0)),
            const(1, D),
            const(*wup.shape),
            const(*cw.shape),
            const(*cb.shape),
            const(*wdn.shape),
            const(1, D),
        ],
        out_specs=pl.BlockSpec((None, tm, D), lambda b, i: (b, i, 0)),
        out_shape=jax.ShapeDtypeStruct((B, S, D), F32),
        scratch_shapes=[pltpu.VMEM((tm, wdn.shape[0]), BF16)],
        compiler_params=pltpu.CompilerParams(
            dimension_semantics=("parallel", "parallel"), vmem_limit_bytes=VMEM_LIMIT),
        name="ffn",
    )(x, x, g, wup, cw, cb, wdn, g_final)


def _ffn_chunk(d_ff):
    for fc in (256, 128):
        if d_ff % fc == 0:
            return fc
    raise ValueError(f"unsupported FFN width {d_ff}")


def kernel(x, g_mix, w_in, g_kv, w_uk, w_uv, w_pool, pool_scale, w_o, g_ffn, w_up, conv_w, conv_b, w_down, g_final):
    B, S, D = x.shape
    depth = g_mix.shape[0]
    d_ff = w_down.shape[1]
    fc = _ffn_chunk(d_ff)
    n_chunks = d_ff // fc
    n_top = min(TOPK_MAX, S // 4)
    tm = min(512, S)
    qb = min(256, S)
    assert S % tm == 0 and S % qb == 0 and qb % KEY_TILE == 0 and tm % KEY_TILE == 0 and KEY_TILE % CHUNK == 0
    assert w_in.shape[2] == COL_Q + KV_LATENT + COL_QI + IDX_HEAD_DIM + N_IDX_HEADS + POOL_WIDTH

    s_q = COL_Q
    s_c = s_q + KV_LATENT
    s_qi = s_c + COL_QI
    s_ki = s_qi + IDX_HEAD_DIM
    s_wi = s_ki + N_IDX_HEADS
    row = lambda v: v.reshape(1, -1).astype(F32)

    for l in range(depth):
        w = w_in[l]
        w_cat = jnp.concatenate(
            [w[:, :s_q], w[:, s_q:s_c], w[:, s_c:s_qi], w[:, s_wi:], w[:, s_qi:s_ki], w[:, s_ki:s_wi],
             jnp.zeros((D, TAIL - IDX_HEAD_DIM - N_IDX_HEADS), w.dtype)], axis=1).astype(BF16)
        wuvt = jnp.swapaxes(w_uv[l], 1, 2).astype(BF16)
        pair = lambda m: jnp.concatenate(
            [m[..., :d_ff].reshape(m.shape[:-1] + (n_chunks, fc)),
             m[..., d_ff:].reshape(m.shape[:-1] + (n_chunks, fc))], axis=-1)
        wup = jnp.moveaxis(pair(w_up[l]), 1, 0).astype(BF16)
        cw = jnp.moveaxis(pair(conv_w[l]), 1, 0).astype(F32)
        cb = pair(conv_b[l]).reshape(n_chunks, 1, 2 * fc).astype(F32)
        wdn = w_down[l].astype(BF16)

        qlt, c, cxt, qit, kx, wit, u = _proj(x, row(g_mix[l]), w_cat, w_uk[l].astype(BF16), row(g_kv[l]), tm)
        a = _attn(qlt, qit, wit, kx, c, cxt, wuvt, qb, n_top)
        x = _mix(x, a, u, w_pool[l].astype(BF16), row(pool_scale[l]), w_o[l].astype(BF16), tm)
        x = _ffn(x, row(g_ffn[l]), wup, cw, cb, wdn, row(g_final), tm, final_norm=(l == depth - 1))
    return x
```

```python
import functools

import jax
import jax.numpy as jnp
from jax import lax
from jax.experimental import pallas as pl
from jax.experimental.pallas import tpu as pltpu

CHUNK = 64
N_ATT_HEADS = 8
ATT_HEAD_DIM = 64
KV_LATENT = 128
N_IDX_HEADS = 8
IDX_HEAD_DIM = 64
TOPK_MAX = 256
POOL_WINDOWS = (2, 4, 8, 16)
POOL_GROUP = 128
CONV_WIDTH = 3
EPS = 1e-6

COL_Q = N_ATT_HEADS * ATT_HEAD_DIM
COL_QI = N_IDX_HEADS * IDX_HEAD_DIM
POOL_WIDTH = POOL_GROUP * len(POOL_WINDOWS)
TAIL = 128
U_HALO = 32
A_HALO = 16
CONV_HALO = 8
MASKED_DIST = 1e30
FIRST_BISECT_STEPS = 14
MORE_BISECT_STEPS = 2
TIE_BISECT_STEPS = 8
MAX_BISECT_ROUNDS = 100
ONES_ROWS = 16
SUBLANES = 8
FOLD_WAYS = 4
KEY_TILE = 256
LOG2E = 1.4426950408889634
VMEM_LIMIT = 56 * 1024 * 1024

F32 = jnp.float32
BF16 = jnp.bfloat16


def _rms(x, g):
    return x * lax.rsqrt(jnp.mean(x * x, axis=-1, keepdims=True) + EPS) * g


def _proj_kernel(x_ref, g_ref, w_ref, wuk_ref, gkv_ref,
                 qlt_ref, c_ref, cxt_ref, qit_ref, kx_ref, wit_ref, u_ref):
    h = _rms(x_ref[...], g_ref[...]).astype(BF16)
    p = jnp.dot(h, w_ref[...], preferred_element_type=F32)
    tm = p.shape[0]
    o_c = COL_Q
    o_qi = o_c + KV_LATENT
    o_u = o_qi + COL_QI
    o_t = o_u + POOL_WIDTH
    q = p[:, :o_c].astype(BF16)
    for hd in range(N_ATT_HEADS):
        ql = jnp.dot(q[:, hd * ATT_HEAD_DIM:(hd + 1) * ATT_HEAD_DIM], wuk_ref[hd],
                     preferred_element_type=F32)
        qlt_ref[hd] = (ql * (ATT_HEAD_DIM ** -0.5 * LOG2E)).T.astype(BF16)
    c = _rms(p[:, o_c:o_qi], gkv_ref[...])
    c_ref[...] = c.astype(BF16)
    ones_rows = jnp.where(lax.broadcasted_iota(jnp.int32, (ONES_ROWS, tm), 0) == 0, 1.0, 0.0)
    cxt = jnp.concatenate([c.T, ones_rows], axis=0).astype(BF16)
    for tt in range(cxt_ref.shape[0]):
        cxt_ref[tt] = cxt[:, tt * KEY_TILE:(tt + 1) * KEY_TILE]
    qit_ref[...] = p[:, o_qi:o_u].T.astype(BF16)
    u_ref[...] = p[:, o_u:o_t]
    tail = p[:, o_t:o_t + TAIL]
    kx_ref[...] = tail[:, :IDX_HEAD_DIM].astype(BF16)
    wit_ref[...] = tail.T[IDX_HEAD_DIM:IDX_HEAD_DIM + N_IDX_HEADS] * (
        (N_IDX_HEADS ** -0.5) * (IDX_HEAD_DIM ** -0.5))


def _proj(x, g, w_cat, wuk, gkv, tm):
    B, S, D = x.shape
    ncol = w_cat.shape[1]
    grid = (B, S // tm)
    const = lambda *shape: pl.BlockSpec(shape, lambda b, i: (0,) * len(shape))
    return pl.pallas_call(
        _proj_kernel,
        grid=grid,
        in_specs=[
            pl.BlockSpec((None, tm, D), lambda b, i: (b, i, 0)),
            const(1, D),
            const(D, ncol),
            const(N_ATT_HEADS, ATT_HEAD_DIM, KV_LATENT),
            const(1, KV_LATENT),
        ],
        out_specs=[
            pl.BlockSpec((None, N_ATT_HEADS, KV_LATENT, tm), lambda b, i: (b, 0, 0, i)),
            pl.BlockSpec((None, tm, KV_LATENT), lambda b, i: (b, i, 0)),
            pl.BlockSpec((None, tm // KEY_TILE, KV_LATENT + ONES_ROWS, KEY_TILE), lambda b, i: (b, i, 0, 0)),
            pl.BlockSpec((None, COL_QI, tm), lambda b, i: (b, 0, i)),
            pl.BlockSpec((None, tm, IDX_HEAD_DIM), lambda b, i: (b, i, 0)),
            pl.BlockSpec((None, N_IDX_HEADS, tm), lambda b, i: (b, 0, i)),
            pl.BlockSpec((None, tm, POOL_WIDTH), lambda b, i: (b, i, 0)),
        ],
        out_shape=[
            jax.ShapeDtypeStruct((B, N_ATT_HEADS, KV_LATENT, S), BF16),
            jax.ShapeDtypeStruct((B, S, KV_LATENT), BF16),
            jax.ShapeDtypeStruct((B, S // KEY_TILE, KV_LATENT + ONES_ROWS, KEY_TILE), BF16),
            jax.ShapeDtypeStruct((B, COL_QI, S), BF16),
            jax.ShapeDtypeStruct((B, S, IDX_HEAD_DIM), BF16),
            jax.ShapeDtypeStruct((B, N_IDX_HEADS, S), F32),
            jax.ShapeDtypeStruct((B, S, POOL_WIDTH), F32),
        ],
        compiler_params=pltpu.CompilerParams(
            dimension_semantics=("parallel", "parallel"), vmem_limit_bytes=VMEM_LIMIT),
        name="proj",
    )(x, g, w_cat, wuk, gkv)


def _any(mask):
    return jnp.max(jnp.where(mask, 1.0, 0.0)) > 0.0


def _fold_rows(x, op, ways=FOLD_WAYS):
    groups = [x[g * SUBLANES:(g + 1) * SUBLANES] for g in range(x.shape[0] // SUBLANES)]
    parts = groups[:ways]
    for g, grp in enumerate(groups[ways:]):
        parts[g % ways] = op(parts[g % ways], grp)
    while len(parts) > 1:
        parts = [op(parts[i], parts[i + 1]) if i + 1 < len(parts) else parts[i] for i in range(0, len(parts), 2)]
    return parts[0]


def _attn_kernel(qlt_ref, qit_ref, wit_ref, kx_ref, c_ref, cxt_ref, wuvt_ref, a_ref,
                 sm_ref, dm_ref, lg_ref, mx_ref, ox_ref, *, n_top):
    qb = qit_ref.shape[1]
    n_tiles, _, tk = cxt_ref.shape
    dc = c_ref.shape[1]
    j = pl.program_id(1)
    nk = (j + 1) * (qb // tk)

    qpos = j * qb + lax.broadcasted_iota(jnp.int32, (1, qb), 1)
    qchunk = qpos // CHUNK
    n_adm = ((qchunk + 1) * CHUNK).astype(F32)
    k_sel = jnp.minimum(n_adm, float(n_top))
    pos_inf = jnp.full((1, qb), jnp.inf, F32)
    group_zeros = jnp.zeros((SUBLANES, qb), F32)
    group_inf = jnp.full((SUBLANES, qb), jnp.inf, F32)

    def kpos_of(t, n=tk):
        return t * n + lax.broadcasted_iota(jnp.int32, (n, 1), 0)

    n_pairs = (nk + 1) // 2

    def half_of(t):
        return pl.ds(pl.multiple_of((t % 2) * tk, tk), tk)

    @pl.when(nk % 2 == 1)
    def _():
        sm_ref[nk // 2, tk:, :] = jnp.full((tk, qb), -jnp.inf, F32)

    def keys_of(ref, t):
        return ref[pl.ds(pl.multiple_of(t * tk, tk), tk), :]

    def col_sum(x):
        return jnp.sum(x, axis=0, keepdims=True)

    wit = wit_ref[...]

    def score_tile(t, carry):
        mn, mx = carry
        kx = keys_of(kx_ref, t)
        score = None
        for i in range(N_IDX_HEADS):
            rel = jnp.dot(kx, qit_ref[i * IDX_HEAD_DIM:(i + 1) * IDX_HEAD_DIM, :],
                          preferred_element_type=F32)
            term = wit[i:i + 1, :] * jnp.maximum(rel, 0.0)
            score = term if score is None else score + term
        adm = (kpos_of(t) // CHUNK) <= qchunk
        sm = jnp.where(adm, score, -jnp.inf)
        sm_ref[t // 2, half_of(t), :] = sm
        return (jnp.minimum(mn, _fold_rows(jnp.where(adm, score, jnp.inf), jnp.minimum)),
                jnp.maximum(mx, _fold_rows(sm, jnp.maximum)))

    mn, mx = lax.fori_loop(0, nk, score_tile, (group_inf, -group_inf))
    q_min = jnp.min(mn, axis=0, keepdims=True)
    q_max = jnp.max(mx, axis=0, keepdims=True)

    state = (q_min, q_max + (jnp.abs(q_max) * (2.0 ** -20) + 1e-30), n_adm, jnp.zeros_like(n_adm))

    def count_ge(mid):
        def body(pr, acc):
            return acc + _fold_rows(jnp.where(sm_ref[pr] >= mid, 1.0, 0.0), jnp.add)
        return col_sum(lax.fori_loop(0, n_pairs, body, group_zeros))

    def bisect(n, st):
        lo, hi, clo, chi = st
        for _ in range(n):
            mid = lo + 0.5 * (hi - lo)
            cnt = count_ge(mid)
            ge = cnt >= k_sel
            lo, hi = jnp.where(ge, mid, lo), jnp.where(ge, hi, mid)
            clo, chi = jnp.where(ge, cnt, clo), jnp.where(ge, chi, cnt)
        return lo, hi, clo, chi

    def in_range_ends(st):
        lo, hi = st[0], st[1]

        def body(pr, carry):
            s = sm_ref[pr]
            return (jnp.minimum(carry[0], _fold_rows(jnp.where(s >= lo, s, jnp.inf), jnp.minimum)),
                    jnp.maximum(carry[1], _fold_rows(jnp.where(s < hi, s, -jnp.inf), jnp.maximum)))

        v_lo, v_hi = lax.fori_loop(0, n_pairs, body, (group_inf, -group_inf))
        return jnp.min(v_lo, axis=0, keepdims=True), jnp.max(v_hi, axis=0, keepdims=True)

    def peel(st):
        lo, hi, clo, chi = st
        v_lo, v_hi = in_range_ends(st)
        done = clo == k_sel
        one_short = jnp.logical_and(jnp.logical_not(done), k_sel - chi == 1.0)
        one_over = jnp.logical_and(jnp.logical_not(jnp.logical_or(done, one_short)), clo - k_sel == 1.0)
        settled = jnp.logical_or(jnp.logical_or(done, one_short), jnp.logical_or(one_over, v_lo == v_hi))
        return (jnp.where(one_short, v_hi, lo), jnp.where(one_over, v_lo, pos_inf),
                _any(jnp.logical_not(settled)))

    state = bisect(FIRST_BISECT_STEPS, state)

    def refine(carry):
        _, rounds, st, _, _ = carry
        lo_sel, drop, is_open = peel(st)
        st = lax.cond(is_open, functools.partial(bisect, MORE_BISECT_STEPS), lambda s: s, st)
        return is_open.astype(jnp.int32), rounds + 1, st, lo_sel, drop

    undecided = _any(state[2] != k_sel)
    _, _, state, lo_sel, drop = lax.while_loop(
        lambda c: jnp.logical_and(c[0] > 0, c[1] < MAX_BISECT_ROUNDS), refine,
        (undecided.astype(jnp.int32), jnp.int32(0), state, state[0], pos_inf))

    def select_pair(pr, acc):
        s = sm_ref[pr]
        sel = jnp.logical_and(s >= lo_sel, s != drop)
        dist = jnp.abs(qpos - kpos_of(pr, 2 * tk)).astype(F32)
        dm_ref[pr] = jnp.where(sel, dist, MASKED_DIST)
        return acc + _fold_rows(jnp.where(sel, 1.0, 0.0), jnp.add)

    n_sel = col_sum(lax.fori_loop(0, n_pairs, select_pair, group_zeros))

    @pl.when(_any(n_sel != k_sel))
    def _():
        def unresolved(st):
            v_lo, v_hi = in_range_ends(st)
            return _any(jnp.logical_and(st[2] != k_sel, v_lo != v_hi)).astype(jnp.int32)

        def tighten(carry):
            _, rounds, st = carry
            st = bisect(TIE_BISECT_STEPS, st)
            return unresolved(st), rounds + 1, st

        _, _, (lo, hi, _, chi) = lax.while_loop(
            lambda c: jnp.logical_and(c[0] > 0, c[1] < MAX_BISECT_ROUNDS), tighten,
            (unresolved(state), jnp.int32(0), state))
        need = k_sel - chi
        key_i = lax.broadcasted_iota(jnp.int32, (tk, tk), 0)
        tri = jnp.where(key_i >= lax.broadcasted_iota(jnp.int32, (tk, tk), 1), 1.0, 0.0).astype(BF16)

        def reselect_tile(t, seen):
            s = sm_ref[t // 2, half_of(t), :]
            in_range = jnp.logical_and(s >= lo, s < hi)
            ones = jnp.where(in_range, 1.0, 0.0)
            rank = seen + jnp.dot(tri, ones.astype(BF16), preferred_element_type=F32)
            sel_x = jnp.logical_or(s >= hi, jnp.logical_and(in_range, rank <= need))
            dm_ref[t // 2, half_of(t), :] = jnp.where(sel_x, jnp.abs(qpos - kpos_of(t)).astype(F32), MASKED_DIST)
            return seen + col_sum(_fold_rows(ones, jnp.add))

        lax.fori_loop(0, nk, reselect_tile, jnp.zeros((1, qb), F32))

    slopes = [LOG2E * 2.0 ** (-8.0 * (hd + 1) / N_ATT_HEADS) for hd in range(N_ATT_HEADS)]
    for hd in range(N_ATT_HEADS):
        mx_ref[hd] = -group_inf
        ox_ref[hd] = jnp.zeros(ox_ref.shape[1:], F32)

    def top_tile(t, _):
        dm = dm_ref[t // 2, half_of(t), :]
        c = keys_of(c_ref, t)
        for hd in range(N_ATT_HEADS):
            logits = jnp.dot(c, qlt_ref[hd], preferred_element_type=F32) - slopes[hd] * dm
            lg_ref[hd, t] = logits
            mx_ref[hd] = jnp.maximum(mx_ref[hd], _fold_rows(logits, jnp.maximum, ways=1))
        return 0

    lax.fori_loop(0, nk, top_tile, 0)
    q_top = [jnp.max(mx_ref[hd], axis=0, keepdims=True) for hd in range(N_ATT_HEADS)]

    def pv_tile(t, _):
        cxt = cxt_ref[t]
        for hd in range(N_ATT_HEADS):
            p = jnp.exp2(lg_ref[hd, t] - q_top[hd]).astype(BF16)
            ox_ref[hd] += jnp.dot(cxt, p, preferred_element_type=F32)
        return 0

    lax.fori_loop(0, nk, pv_tile, 0)
    a_heads = []
    for hd in range(N_ATT_HEADS):
        ox = ox_ref[hd]
        o_t = (ox[:dc] / ox[dc:dc + 1]).astype(BF16)
        a_heads.append(jnp.dot(wuvt_ref[hd], o_t, preferred_element_type=F32))
    a_ref[...] = jnp.concatenate(a_heads, axis=0).T.astype(BF16)


def _attn(qlt, qit, wit, kx, c, cxt, wuvt, qb, n_top):
    B, H, dc, S = qlt.shape
    n_tiles, ox_rows, tk = cxt.shape[1:]
    grid = (B, S // qb)
    return pl.pallas_call(
        functools.partial(_attn_kernel, n_top=n_top),
        grid=grid,
        in_specs=[
            pl.BlockSpec((None, H, dc, qb), lambda b, j: (b, 0, 0, j)),
            pl.BlockSpec((None, COL_QI, qb), lambda b, j: (b, 0, j)),
            pl.BlockSpec((None, N_IDX_HEADS, qb), lambda b, j: (b, 0, j)),
            pl.BlockSpec((None, S, IDX_HEAD_DIM), lambda b, j: (b, 0, 0)),
            pl.BlockSpec((None, S, dc), lambda b, j: (b, 0, 0)),
            pl.BlockSpec((None, n_tiles, ox_rows, tk), lambda b, j: (b, 0, 0, 0)),
            pl.BlockSpec((H, ATT_HEAD_DIM, dc), lambda b, j: (0, 0, 0)),
        ],
        out_specs=pl.BlockSpec((None, qb, COL_Q), lambda b, j: (b, j, 0)),
        out_shape=jax.ShapeDtypeStruct((B, S, COL_Q), BF16),
        scratch_shapes=[
            pltpu.VMEM(((n_tiles + 1) // 2, 2 * tk, qb), F32),
            pltpu.VMEM(((n_tiles + 1) // 2, 2 * tk, qb), F32),
            pltpu.VMEM((H, n_tiles, tk, qb), F32),
            pltpu.VMEM((H, SUBLANES, qb), F32),
            pltpu.VMEM((H, ox_rows, qb), F32),
        ],
        compiler_params=pltpu.CompilerParams(
            dimension_semantics=("parallel", "parallel"), vmem_limit_bytes=VMEM_LIMIT),
        name="attn",
    )(qlt, qit, wit, kx, c, cxt, wuvt)


def _mixffn_kernel(x_ref, xh_ref, a_ref, ah_ref, u_ref, uh_ref, wp_ref, ps_ref, wo_ref,
                   g_ref, wup_ref, cw_ref, cb_ref, wdn_ref, gf_ref, o_ref, act_ref, *, final_norm):
    tm, D = x_ref.shape
    n_chunks, _, fc2 = wup_ref.shape
    fc = fc2 // 2
    i = pl.program_id(1)
    rows = CONV_HALO + tm
    first = i == 0

    u_hist = jnp.where(first, 0.0, uh_ref[...])
    ext = jnp.concatenate([u_hist, u_ref[...]], axis=0)
    t = i * tm - CONV_HALO + lax.broadcasted_iota(jnp.int32, (rows, 1), 0)
    mixed = []
    for g, win in enumerate(POOL_WINDOWS):
        e = ext[:, g * POOL_GROUP:(g + 1) * POOL_GROUP]
        acc = e
        span = 1
        while span < win:
            acc = acc + pltpu.roll(acc, span, axis=0)
            span *= 2
        count = jnp.clip(t + 1, 1, win).astype(F32)
        pooled = acc[U_HALO - CONV_HALO:] / count - e[U_HALO - CONV_HALO:]
        mixed.append(jnp.dot(pooled.astype(BF16), wp_ref[g], preferred_element_type=F32))
    b = (jnp.concatenate(mixed, axis=1) * ps_ref[...]).astype(BF16)
    a = jnp.concatenate([ah_ref[A_HALO - CONV_HALO:], a_ref[...]], axis=0)
    xe = jnp.concatenate([xh_ref[...], x_ref[...]], axis=0)
    xe = xe + jnp.dot(jnp.concatenate([a, b], axis=1), wo_ref[...], preferred_element_type=F32)
    x = xe[CONV_HALO:]

    h = _rms(xe, g_ref[...])
    row = lax.broadcasted_iota(jnp.int32, (rows, 1), 0)
    h = jnp.where(jnp.logical_and(first, row < CONV_HALO), 0.0, h).astype(BF16)

    for k in range(n_chunks):
        up = jnp.dot(h, wup_ref[k], preferred_element_type=F32)
        cw = cw_ref[k]
        conv = cb_ref[k] + cw[2:3] * up
        for jj in range(CONV_WIDTH - 1):
            conv = conv + cw[jj:jj + 1] * pltpu.roll(up, CONV_WIDTH - 1 - jj, axis=0)
        conv = conv[CONV_HALO:]
        gate = conv[:, :fc]
        act = gate * (1.0 / (1.0 + jnp.exp(-gate))) * conv[:, fc:]
        act_ref[:, k * fc:(k + 1) * fc] = act.astype(BF16)

    y = x + jnp.dot(act_ref[...], wdn_ref[...], preferred_element_type=F32)
    if final_norm:
        y = _rms(y, gf_ref[...])
    o_ref[...] = y


def _mixffn(x, a, u, w_pool, pool_scale, w_o, g, wup, cw, cb, wdn, g_final, tm, final_norm):
    B, S, D = x.shape
    grid = (B, S // tm)
    hist = lambda rows: (lambda b, i: (b, jnp.maximum(i * (tm // rows) - 1, 0), 0))
    tile = lambda b, i: (b, i, 0)
    const = lambda *shape: pl.BlockSpec(shape, lambda b, i: (0,) * len(shape), pipeline_mode=pl.Buffered(1))
    return pl.pallas_call(
        functools.partial(_mixffn_kernel, final_norm=final_norm),
        grid=grid,
        in_specs=[
            pl.BlockSpec((None, tm, D), tile),
            pl.BlockSpec((None, CONV_HALO, D), hist(CONV_HALO)),
            pl.BlockSpec((None, tm, COL_Q), tile),
            pl.BlockSpec((None, A_HALO, COL_Q), hist(A_HALO)),
            pl.BlockSpec((None, tm, POOL_WIDTH), tile),
            pl.BlockSpec((None, U_HALO, POOL_WIDTH), hist(U_HALO)),
            const(*w_pool.shape),
            const(1, POOL_WIDTH),
            const(*w_o.shape),
            const(1, D),
            const(*wup.shape),
            const(*cw.shape),
            const(*cb.shape),
            const(*wdn.shape),
            const(1, D),
        ],
        out_specs=pl.BlockSpec((None, tm, D), tile),
        out_shape=jax.ShapeDtypeStruct((B, S, D), F32),
        scratch_shapes=[pltpu.VMEM((tm, wdn.shape[0]), BF16)],
        compiler_params=pltpu.CompilerParams(
            dimension_semantics=("parallel", "parallel"), vmem_limit_bytes=VMEM_LIMIT),
        name="mixffn",
    )(x, x, a, a, u, u, w_pool, pool_scale, w_o, g, wup, cw, cb, wdn, g_final)


def _ffn_chunk(d_ff):
    for fc in (256, 128):
        if d_ff % fc == 0:
            return fc
    raise ValueError(f"unsupported FFN width {d_ff}")


def kernel(x, g_mix, w_in, g_kv, w_uk, w_uv, w_pool, pool_scale, w_o, g_ffn, w_up, conv_w, conv_b, w_down, g_final):
    B, S, D = x.shape
    depth = g_mix.shape[0]
    d_ff = w_down.shape[1]
    fc = _ffn_chunk(d_ff)
    n_chunks = d_ff // fc
    n_top = min(TOPK_MAX, S // 4)
    tm = min(512, S)
    qb = min(256, S)
    assert S % tm == 0 and S % qb == 0 and qb % KEY_TILE == 0 and tm % KEY_TILE == 0 and KEY_TILE % CHUNK == 0
    assert w_in.shape[2] == COL_Q + KV_LATENT + COL_QI + IDX_HEAD_DIM + N_IDX_HEADS + POOL_WIDTH

    s_q = COL_Q
    s_c = s_q + KV_LATENT
    s_qi = s_c + COL_QI
    s_ki = s_qi + IDX_HEAD_DIM
    s_wi = s_ki + N_IDX_HEADS
    row = lambda v: v.reshape(1, -1).astype(F32)

    for l in range(depth):
        w = w_in[l]
        w_cat = jnp.concatenate(
            [w[:, :s_q], w[:, s_q:s_c], w[:, s_c:s_qi], w[:, s_wi:], w[:, s_qi:s_ki], w[:, s_ki:s_wi],
             jnp.zeros((D, TAIL - IDX_HEAD_DIM - N_IDX_HEADS), w.dtype)], axis=1).astype(BF16)
        wuvt = jnp.swapaxes(w_uv[l], 1, 2).astype(BF16)
        pair = lambda m: jnp.concatenate(
            [m[..., :d_ff].reshape(m.shape[:-1] + (n_chunks, fc)),
             m[..., d_ff:].reshape(m.shape[:-1] + (n_chunks, fc))], axis=-1)
        wup = jnp.moveaxis(pair(w_up[l]), 1, 0).astype(BF16)
        cw = jnp.moveaxis(pair(conv_w[l]), 1, 0).astype(F32)
        cb = pair(conv_b[l]).reshape(n_chunks, 1, 2 * fc).astype(F32)
        wdn = w_down[l].astype(BF16)

        qlt, c, cxt, qit, kx, wit, u = _proj(x, row(g_mix[l]), w_cat, w_uk[l].astype(BF16), row(g_kv[l]), tm)
        a = _attn(qlt, qit, wit, kx, c, cxt, wuvt, qb, n_top)
        x = _mixffn(x, a, u, w_pool[l].astype(BF16), row(pool_scale[l]), w_o[l].astype(BF16),
                    row(g_ffn[l]), wup, cw, cb, wdn, row(g_final), tm, final_norm=(l == depth - 1))
    return x
```

```python
import functools

import jax
import jax.numpy as jnp
from jax import lax
from jax.experimental import pallas as pl
from jax.experimental.pallas import tpu as pltpu

CHUNK = 64
N_ATT_HEADS = 8
ATT_HEAD_DIM = 64
KV_LATENT = 128
N_IDX_HEADS = 8
IDX_HEAD_DIM = 64
TOPK_MAX = 256
POOL_WINDOWS = (2, 4, 8, 16)
POOL_GROUP = 128
CONV_WIDTH = 3
EPS = 1e-6

COL_Q = N_ATT_HEADS * ATT_HEAD_DIM
COL_QI = N_IDX_HEADS * IDX_HEAD_DIM
POOL_WIDTH = POOL_GROUP * len(POOL_WINDOWS)
TAIL = 128
U_HALO = 32
A_HALO = 16
CONV_HALO = 8
MASKED_DIST = 1e30
FIRST_BISECT_STEPS = 14
MORE_BISECT_STEPS = 2
TIE_BISECT_STEPS = 8
MAX_BISECT_ROUNDS = 100
ONES_ROWS = 16
SUBLANES = 8
FOLD_WAYS = 4
KEY_TILE = 256
LOG2E = 1.4426950408889634
VMEM_LIMIT = 56 * 1024 * 1024

F32 = jnp.float32
BF16 = jnp.bfloat16


def _rms(x, g):
    return x * lax.rsqrt(jnp.mean(x * x, axis=-1, keepdims=True) + EPS) * g


def _proj_kernel(x_ref, g_ref, w_ref, wuk_ref, gkv_ref,
                 qlt_ref, c_ref, cxt_ref, qit_ref, kx_ref, wit_ref, u_ref):
    h = _rms(x_ref[...], g_ref[...]).astype(BF16)
    p = jnp.dot(h, w_ref[...], preferred_element_type=F32)
    tm = p.shape[0]
    o_c = COL_Q
    o_qi = o_c + KV_LATENT
    o_u = o_qi + COL_QI
    o_t = o_u + POOL_WIDTH
    q = p[:, :o_c].astype(BF16)
    for hd in range(N_ATT_HEADS):
        ql = jnp.dot(q[:, hd * ATT_HEAD_DIM:(hd + 1) * ATT_HEAD_DIM], wuk_ref[hd],
                     preferred_element_type=F32)
        qlt_ref[hd] = (ql * (ATT_HEAD_DIM ** -0.5 * LOG2E)).T.astype(BF16)
    c = _rms(p[:, o_c:o_qi], gkv_ref[...])
    c_ref[...] = c.astype(BF16)
    ones_rows = jnp.where(lax.broadcasted_iota(jnp.int32, (ONES_ROWS, tm), 0) == 0, 1.0, 0.0)
    cxt = jnp.concatenate([c.T, ones_rows], axis=0).astype(BF16)
    for tt in range(cxt_ref.shape[0]):
        cxt_ref[tt] = cxt[:, tt * KEY_TILE:(tt + 1) * KEY_TILE]
    qit_ref[...] = p[:, o_qi:o_u].T.astype(BF16)
    u_ref[...] = p[:, o_u:o_t]
    tail = p[:, o_t:o_t + TAIL]
    kx_ref[...] = tail[:, :IDX_HEAD_DIM].astype(BF16)
    wit_ref[...] = tail.T[IDX_HEAD_DIM:IDX_HEAD_DIM + N_IDX_HEADS] * (
        (N_IDX_HEADS ** -0.5) * (IDX_HEAD_DIM ** -0.5))


def _proj(x, g, w_cat, wuk, gkv, tm):
    B, S, D = x.shape
    ncol = w_cat.shape[1]
    grid = (B, S // tm)
    const = lambda *shape: pl.BlockSpec(shape, lambda b, i: (0,) * len(shape))
    return pl.pallas_call(
        _proj_kernel,
        grid=grid,
        in_specs=[
            pl.BlockSpec((None, tm, D), lambda b, i: (b, i, 0)),
            const(1, D),
            const(D, ncol),
            const(N_ATT_HEADS, ATT_HEAD_DIM, KV_LATENT),
            const(1, KV_LATENT),
        ],
        out_specs=[
            pl.BlockSpec((None, N_ATT_HEADS, KV_LATENT, tm), lambda b, i: (b, 0, 0, i)),
            pl.BlockSpec((None, tm, KV_LATENT), lambda b, i: (b, i, 0)),
            pl.BlockSpec((None, tm // KEY_TILE, KV_LATENT + ONES_ROWS, KEY_TILE), lambda b, i: (b, i, 0, 0)),
            pl.BlockSpec((None, COL_QI, tm), lambda b, i: (b, 0, i)),
            pl.BlockSpec((None, tm, IDX_HEAD_DIM), lambda b, i: (b, i, 0)),
            pl.BlockSpec((None, N_IDX_HEADS, tm), lambda b, i: (b, 0, i)),
            pl.BlockSpec((None, tm, POOL_WIDTH), lambda b, i: (b, i, 0)),
        ],
        out_shape=[
            jax.ShapeDtypeStruct((B, N_ATT_HEADS, KV_LATENT, S), BF16),
            jax.ShapeDtypeStruct((B, S, KV_LATENT), BF16),
            jax.ShapeDtypeStruct((B, S // KEY_TILE, KV_LATENT + ONES_ROWS, KEY_TILE), BF16),
            jax.ShapeDtypeStruct((B, COL_QI, S), BF16),
            jax.ShapeDtypeStruct((B, S, IDX_HEAD_DIM), BF16),
            jax.ShapeDtypeStruct((B, N_IDX_HEADS, S), F32),
            jax.ShapeDtypeStruct((B, S, POOL_WIDTH), F32),
        ],
        compiler_params=pltpu.CompilerParams(
            dimension_semantics=("parallel", "parallel"), vmem_limit_bytes=VMEM_LIMIT),
        name="proj",
    )(x, g, w_cat, wuk, gkv)


def _any(mask):
    return jnp.max(jnp.where(mask, 1.0, 0.0)) > 0.0


def _fold_rows(x, op, ways=FOLD_WAYS):
    groups = [x[g * SUBLANES:(g + 1) * SUBLANES] for g in range(x.shape[0] // SUBLANES)]
    parts = groups[:ways]
    for g, grp in enumerate(groups[ways:]):
        parts[g % ways] = op(parts[g % ways], grp)
    while len(parts) > 1:
        parts = [op(parts[i], parts[i + 1]) if i + 1 < len(parts) else parts[i] for i in range(0, len(parts), 2)]
    return parts[0]


def _attn_kernel(qlt_ref, qit_ref, wit_ref, kx_ref, c_ref, cxt_ref, wuvt_ref, a_ref,
                 sm_ref, dm_ref, lg_ref, mx_ref, ox_ref, *, n_top):
    qb = qit_ref.shape[1]
    n_tiles, _, tk = cxt_ref.shape
    dc = c_ref.shape[1]
    j = pl.program_id(1)
    nk = (j + 1) * (qb // tk)

    qpos = j * qb + lax.broadcasted_iota(jnp.int32, (1, qb), 1)
    qchunk = qpos // CHUNK
    n_adm = ((qchunk + 1) * CHUNK).astype(F32)
    k_sel = jnp.minimum(n_adm, float(n_top))
    pos_inf = jnp.full((1, qb), jnp.inf, F32)
    group_zeros = jnp.zeros((SUBLANES, qb), F32)
    group_inf = jnp.full((SUBLANES, qb), jnp.inf, F32)

    def kpos_of(t, n=tk):
        return t * n + lax.broadcasted_iota(jnp.int32, (n, 1), 0)

    n_pairs = (nk + 1) // 2

    def half_of(t):
        return pl.ds(pl.multiple_of((t % 2) * tk, tk), tk)

    @pl.when(nk % 2 == 1)
    def _():
        sm_ref[nk // 2, tk:, :] = jnp.full((tk, qb), -jnp.inf, F32)

    def keys_of(ref, t):
        return ref[pl.ds(pl.multiple_of(t * tk, tk), tk), :]

    def col_sum(x):
        return jnp.sum(x, axis=0, keepdims=True)

    wit = wit_ref[...]

    def score_tile(t, carry):
        mn, mx = carry
        kx = keys_of(kx_ref, t)
        score = None
        for i in range(N_IDX_HEADS):
            rel = jnp.dot(kx, qit_ref[i * IDX_HEAD_DIM:(i + 1) * IDX_HEAD_DIM, :],
                          preferred_element_type=F32)
            term = wit[i:i + 1, :] * jnp.maximum(rel, 0.0)
            score = term if score is None else score + term
        adm = (kpos_of(t) // CHUNK) <= qchunk
        sm = jnp.where(adm, score, -jnp.inf)
        sm_ref[t // 2, half_of(t), :] = sm
        return (jnp.minimum(mn, _fold_rows(jnp.where(adm, score, jnp.inf), jnp.minimum)),
                jnp.maximum(mx, _fold_rows(sm, jnp.maximum)))

    mn, mx = lax.fori_loop(0, nk, score_tile, (group_inf, -group_inf))
    q_min = jnp.min(mn, axis=0, keepdims=True)
    q_max = jnp.max(mx, axis=0, keepdims=True)

    state = (q_min, q_max + (jnp.abs(q_max) * (2.0 ** -20) + 1e-30), n_adm, jnp.zeros_like(n_adm))

    def count_ge(mid):
        def body(pr, acc):
            return acc + _fold_rows(jnp.where(sm_ref[pr] >= mid, 1.0, 0.0), jnp.add)
        return col_sum(lax.fori_loop(0, n_pairs, body, group_zeros))

    def bisect(n, st):
        lo, hi, clo, chi = st
        for _ in range(n):
            mid = lo + 0.5 * (hi - lo)
            cnt = count_ge(mid)
            ge = cnt >= k_sel
            lo, hi = jnp.where(ge, mid, lo), jnp.where(ge, hi, mid)
            clo, chi = jnp.where(ge, cnt, clo), jnp.where(ge, chi, cnt)
        return lo, hi, clo, chi

    def in_range_ends(st):
        lo, hi = st[0], st[1]

        def body(pr, carry):
            s = sm_ref[pr]
            return (jnp.minimum(carry[0], _fold_rows(jnp.where(s >= lo, s, jnp.inf), jnp.minimum)),
                    jnp.maximum(carry[1], _fold_rows(jnp.where(s < hi, s, -jnp.inf), jnp.maximum)))

        v_lo, v_hi = lax.fori_loop(0, n_pairs, body, (group_inf, -group_inf))
        return jnp.min(v_lo, axis=0, keepdims=True), jnp.max(v_hi, axis=0, keepdims=True)

    def peel(st):
        lo, hi, clo, chi = st
        v_lo, v_hi = in_range_ends(st)
        done = clo == k_sel
        one_short = jnp.logical_and(jnp.logical_not(done), k_sel - chi == 1.0)
        one_over = jnp.logical_and(jnp.logical_not(jnp.logical_or(done, one_short)), clo - k_sel == 1.0)
        settled = jnp.logical_or(jnp.logical_or(done, one_short), jnp.logical_or(one_over, v_lo == v_hi))
        return (jnp.where(one_short, v_hi, lo), jnp.where(one_over, v_lo, pos_inf),
                _any(jnp.logical_not(settled)))

    state = bisect(FIRST_BISECT_STEPS, state)

    def refine(carry):
        _, rounds, st, _, _ = carry
        lo_sel, drop, is_open = peel(st)
        st = lax.cond(is_open, functools.partial(bisect, MORE_BISECT_STEPS), lambda s: s, st)
        return is_open.astype(jnp.int32), rounds + 1, st, lo_sel, drop

    undecided = _any(state[2] != k_sel)
    _, _, state, lo_sel, drop = lax.while_loop(
        lambda c: jnp.logical_and(c[0] > 0, c[1] < MAX_BISECT_ROUNDS), refine,
        (undecided.astype(jnp.int32), jnp.int32(0), state, state[0], pos_inf))

    def select_pair(pr, acc):
        s = sm_ref[pr]
        sel = jnp.logical_and(s >= lo_sel, s != drop)
        dist = jnp.abs(qpos - kpos_of(pr, 2 * tk)).astype(F32)
        dm_ref[pr] = jnp.where(sel, dist, MASKED_DIST)
        return acc + _fold_rows(jnp.where(sel, 1.0, 0.0), jnp.add)

    n_sel = col_sum(lax.fori_loop(0, n_pairs, select_pair, group_zeros))

    wrong = n_sel != k_sel

    @pl.when(_any(wrong))
    def _():
        def unresolved(st):
            v_lo, v_hi = in_range_ends(st)
            open_q = jnp.logical_and(wrong, jnp.logical_and(st[2] != k_sel, v_lo != v_hi))
            return _any(open_q).astype(jnp.int32)

        def tighten(carry):
            _, rounds, st = carry
            st = bisect(TIE_BISECT_STEPS, st)
            return unresolved(st), rounds + 1, st

        _, _, (lo, hi, _, chi) = lax.while_loop(
            lambda c: jnp.logical_and(c[0] > 0, c[1] < MAX_BISECT_ROUNDS), tighten,
            (unresolved(state), jnp.int32(0), state))
        need = k_sel - chi
        key_i = lax.broadcasted_iota(jnp.int32, (tk, tk), 0)
        tri = jnp.where(key_i >= lax.broadcasted_iota(jnp.int32, (tk, tk), 1), 1.0, 0.0).astype(BF16)

        def reselect_tile(t, seen):
            s = sm_ref[t // 2, half_of(t), :]
            in_range = jnp.logical_and(s >= lo, s < hi)
            ones = jnp.where(in_range, 1.0, 0.0)
            rank = seen + jnp.dot(tri, ones.astype(BF16), preferred_element_type=F32)
            sel_x = jnp.logical_or(s >= hi, jnp.logical_and(in_range, rank <= need))
            exact = jnp.where(sel_x, jnp.abs(qpos - kpos_of(t)).astype(F32), MASKED_DIST)
            dm_ref[t // 2, half_of(t), :] = jnp.where(wrong, exact, dm_ref[t // 2, half_of(t), :])
            return seen + col_sum(_fold_rows(ones, jnp.add))

        lax.fori_loop(0, nk, reselect_tile, jnp.zeros((1, qb), F32))

    slopes = [LOG2E * 2.0 ** (-8.0 * (hd + 1) / N_ATT_HEADS) for hd in range(N_ATT_HEADS)]
    for hd in range(N_ATT_HEADS):
        mx_ref[hd] = -group_inf
        ox_ref[hd] = jnp.zeros(ox_ref.shape[1:], F32)

    def top_tile(t, _):
        dm = dm_ref[t // 2, half_of(t), :]
        c = keys_of(c_ref, t)
        for hd in range(N_ATT_HEADS):
            logits = jnp.dot(c, qlt_ref[hd], preferred_element_type=F32) - slopes[hd] * dm
            lg_ref[hd, t] = logits
            mx_ref[hd] = jnp.maximum(mx_ref[hd], _fold_rows(logits, jnp.maximum, ways=1))
        return 0

    lax.fori_loop(0, nk, top_tile, 0)
    q_top = [jnp.max(mx_ref[hd], axis=0, keepdims=True) for hd in range(N_ATT_HEADS)]

    def pv_tile(t, _):
        cxt = cxt_ref[t]
        for hd in range(N_ATT_HEADS):
            p = jnp.exp2(lg_ref[hd, t] - q_top[hd]).astype(BF16)
            ox_ref[hd] += jnp.dot(cxt, p, preferred_element_type=F32)
        return 0

    lax.fori_loop(0, nk, pv_tile, 0)
    a_heads = []
    for hd in range(N_ATT_HEADS):
        ox = ox_ref[hd]
        o_t = (ox[:dc] / ox[dc:dc + 1]).astype(BF16)
        a_heads.append(jnp.dot(wuvt_ref[hd], o_t, preferred_element_type=F32))
    a_ref[...] = jnp.concatenate(a_heads, axis=0).T.astype(BF16)


def _attn(qlt, qit, wit, kx, c, cxt, wuvt, qb, n_top):
    B, H, dc, S = qlt.shape
    n_tiles, ox_rows, tk = cxt.shape[1:]
    grid = (B, S // qb)
    return pl.pallas_call(
        functools.partial(_attn_kernel, n_top=n_top),
        grid=grid,
        in_specs=[
            pl.BlockSpec((None, H, dc, qb), lambda b, j: (b, 0, 0, j)),
            pl.BlockSpec((None, COL_QI, qb), lambda b, j: (b, 0, j)),
            pl.BlockSpec((None, N_IDX_HEADS, qb), lambda b, j: (b, 0, j)),
            pl.BlockSpec((None, S, IDX_HEAD_DIM), lambda b, j: (b, 0, 0)),
            pl.BlockSpec((None, S, dc), lambda b, j: (b, 0, 0)),
            pl.BlockSpec((None, n_tiles, ox_rows, tk), lambda b, j: (b, 0, 0, 0)),
            pl.BlockSpec((H, ATT_HEAD_DIM, dc), lambda b, j: (0, 0, 0)),
        ],
        out_specs=pl.BlockSpec((None, qb, COL_Q), lambda b, j: (b, j, 0)),
        out_shape=jax.ShapeDtypeStruct((B, S, COL_Q), BF16),
        scratch_shapes=[
            pltpu.VMEM(((n_tiles + 1) // 2, 2 * tk, qb), F32),
            pltpu.VMEM(((n_tiles + 1) // 2, 2 * tk, qb), F32),
            pltpu.VMEM((H, n_tiles, tk, qb), F32),
            pltpu.VMEM((H, SUBLANES, qb), F32),
            pltpu.VMEM((H, ox_rows, qb), F32),
        ],
        compiler_params=pltpu.CompilerParams(
            dimension_semantics=("parallel", "parallel"), vmem_limit_bytes=VMEM_LIMIT),
        name="attn",
    )(qlt, qit, wit, kx, c, cxt, wuvt)


def _mixffn_kernel(x_ref, xh_ref, a_ref, ah_ref, u_ref, uh_ref, wp_ref, ps_ref, wo_ref,
                   g_ref, wup_ref, cw_ref, cb_ref, wdn_ref, gf_ref, o_ref, act_ref, *, final_norm):
    tm, D = x_ref.shape
    n_chunks, _, fc2 = wup_ref.shape
    fc = fc2 // 2
    i = pl.program_id(1)
    rows = CONV_HALO + tm
    first = i == 0

    u_hist = jnp.where(first, 0.0, uh_ref[...])
    ext = jnp.concatenate([u_hist, u_ref[...]], axis=0)
    t = i * tm - CONV_HALO + lax.broadcasted_iota(jnp.int32, (rows, 1), 0)
    mixed = []
    for g, win in enumerate(POOL_WINDOWS):
        e = ext[:, g * POOL_GROUP:(g + 1) * POOL_GROUP]
        acc = e
        span = 1
        while span < win:
            acc = acc + pltpu.roll(acc, span, axis=0)
            span *= 2
        count = jnp.clip(t + 1, 1, win).astype(F32)
        pooled = acc[U_HALO - CONV_HALO:] / count - e[U_HALO - CONV_HALO:]
        mixed.append(jnp.dot(pooled.astype(BF16), wp_ref[g], preferred_element_type=F32))
    b = (jnp.concatenate(mixed, axis=1) * ps_ref[...]).astype(BF16)
    a = jnp.concatenate([ah_ref[A_HALO - CONV_HALO:], a_ref[...]], axis=0)
    xe = jnp.concatenate([xh_ref[...], x_ref[...]], axis=0)
    xe = xe + jnp.dot(jnp.concatenate([a, b], axis=1), wo_ref[...], preferred_element_type=F32)
    x = xe[CONV_HALO:]

    h = _rms(xe, g_ref[...])
    row = lax.broadcasted_iota(jnp.int32, (rows, 1), 0)
    h = jnp.where(jnp.logical_and(first, row < CONV_HALO), 0.0, h).astype(BF16)

    for k in range(n_chunks):
        up = jnp.dot(h, wup_ref[k], preferred_element_type=F32)
        cw = cw_ref[k]
        conv = cb_ref[k] + cw[2:3] * up
        for jj in range(CONV_WIDTH - 1):
            conv = conv + cw[jj:jj + 1] * pltpu.roll(up, CONV_WIDTH - 1 - jj, axis=0)
        conv = conv[CONV_HALO:]
        gate = conv[:, :fc]
        act = gate * (1.0 / (1.0 + jnp.exp(-gate))) * conv[:, fc:]
        act_ref[:, k * fc:(k + 1) * fc] = act.astype(BF16)

    y = x + jnp.dot(act_ref[...], wdn_ref[...], preferred_element_type=F32)
    if final_norm:
        y = _rms(y, gf_ref[...])
    o_ref[...] = y


def _mixffn(x, a, u, w_pool, pool_scale, w_o, g, wup, cw, cb, wdn, g_final, tm, final_norm):
    B, S, D = x.shape
    grid = (B, S // tm)
    hist = lambda rows: (lambda b, i: (b, jnp.maximum(i * (tm // rows) - 1, 0), 0))
    tile = lambda b, i: (b, i, 0)
    const = lambda *shape: pl.BlockSpec(shape, lambda b, i: (0,) * len(shape), pipeline_mode=pl.Buffered(1))
    return pl.pallas_call(
        functools.partial(_mixffn_kernel, final_norm=final_norm),
        grid=grid,
        in_specs=[
            pl.BlockSpec((None, tm, D), tile),
            pl.BlockSpec((None, CONV_HALO, D), hist(CONV_HALO)),
            pl.BlockSpec((None, tm, COL_Q), tile),
            pl.BlockSpec((None, A_HALO, COL_Q), hist(A_HALO)),
            pl.BlockSpec((None, tm, POOL_WIDTH), tile),
            pl.BlockSpec((None, U_HALO, POOL_WIDTH), hist(U_HALO)),
            const(*w_pool.shape),
            const(1, POOL_WIDTH),
            const(*w_o.shape),
            const(1, D),
            const(*wup.shape),
            const(*cw.shape),
            const(*cb.shape),
            const(*wdn.shape),
            const(1, D),
        ],
        out_specs=pl.BlockSpec((None, tm, D), tile),
        out_shape=jax.ShapeDtypeStruct((B, S, D), F32),
        scratch_shapes=[pltpu.VMEM((tm, wdn.shape[0]), BF16)],
        compiler_params=pltpu.CompilerParams(
            dimension_semantics=("parallel", "parallel"), vmem_limit_bytes=VMEM_LIMIT),
        name="mixffn",
    )(x, x, a, a, u, u, w_pool, pool_scale, w_o, g, wup, cw, cb, wdn, g_final)


def _ffn_chunk(d_ff):
    for fc in (256, 128):
        if d_ff % fc == 0:
            return fc
    raise ValueError(f"unsupported FFN width {d_ff}")


def kernel(x, g_mix, w_in, g_kv, w_uk, w_uv, w_pool, pool_scale, w_o, g_ffn, w_up, conv_w, conv_b, w_down, g_final):
    B, S, D = x.shape
    depth = g_mix.shape[0]
    d_ff = w_down.shape[1]
    fc = _ffn_chunk(d_ff)
    n_chunks = d_ff // fc
    n_top = min(TOPK_MAX, S // 4)
    tm = min(512, S)
    qb = min(256, S)
    assert S % tm == 0 and S % qb == 0 and qb % KEY_TILE == 0 and tm % KEY_TILE == 0 and KEY_TILE % CHUNK == 0
    assert w_in.shape[2] == COL_Q + KV_LATENT + COL_QI + IDX_HEAD_DIM + N_IDX_HEADS + POOL_WIDTH

    s_q = COL_Q
    s_c = s_q + KV_LATENT
    s_qi = s_c + COL_QI
    s_ki = s_qi + IDX_HEAD_DIM
    s_wi = s_ki + N_IDX_HEADS
    row = lambda v: v.reshape(1, -1).astype(F32)

    for l in range(depth):
        w = w_in[l]
        w_cat = jnp.concatenate(
            [w[:, :s_q], w[:, s_q:s_c], w[:, s_c:s_qi], w[:, s_wi:], w[:, s_qi:s_ki], w[:, s_ki:s_wi],
             jnp.zeros((D, TAIL - IDX_HEAD_DIM - N_IDX_HEADS), w.dtype)], axis=1).astype(BF16)
        wuvt = jnp.swapaxes(w_uv[l], 1, 2).astype(BF16)
        pair = lambda m: jnp.concatenate(
            [m[..., :d_ff].reshape(m.shape[:-1] + (n_chunks, fc)),
             m[..., d_ff:].reshape(m.shape[:-1] + (n_chunks, fc))], axis=-1)
        wup = jnp.moveaxis(pair(w_up[l]), 1, 0).astype(BF16)
        cw = jnp.moveaxis(pair(conv_w[l]), 1, 0).astype(F32)
        cb = pair(conv_b[l]).reshape(n_chunks, 1, 2 * fc).astype(F32)
        wdn = w_down[l].astype(BF16)

        qlt, c, cxt, qit, kx, wit, u = _proj(x, row(g_mix[l]), w_cat, w_uk[l].astype(BF16), row(g_kv[l]), tm)
        a = _attn(qlt, qit, wit, kx, c, cxt, wuvt, qb, n_top)
        x = _mixffn(x, a, u, w_pool[l].astype(BF16), row(pool_scale[l]), w_o[l].astype(BF16),
                    row(g_ffn[l]), wup, cw, cb, wdn, row(g_final), tm, final_norm=(l == depth - 1))
    return x
```

```python
import functools

import jax
import jax.numpy as jnp
from jax import lax
from jax.experimental import pallas as pl
from jax.experimental.pallas import tpu as pltpu

CHUNK = 64
N_ATT_HEADS = 8
ATT_HEAD_DIM = 64
KV_LATENT = 128
N_IDX_HEADS = 8
IDX_HEAD_DIM = 64
TOPK_MAX = 256
POOL_WINDOWS = (2, 4, 8, 16)
POOL_GROUP = 128
CONV_WIDTH = 3
EPS = 1e-6

COL_Q = N_ATT_HEADS * ATT_HEAD_DIM
COL_QI = N_IDX_HEADS * IDX_HEAD_DIM
POOL_WIDTH = POOL_GROUP * len(POOL_WINDOWS)
TAIL = 128
U_HALO = 32
A_HALO = 16
CONV_HALO = 8
MASKED_DIST = 1e30
FIRST_BISECT_STEPS = 14
MORE_BISECT_STEPS = 2
TIE_BISECT_STEPS = 8
MAX_BISECT_ROUNDS = 100
ONES_ROWS = 16
SUBLANES = 8
FOLD_WAYS = 4
KEY_TILE = 256
LOG2E = 1.4426950408889634
VMEM_LIMIT = 56 * 1024 * 1024

F32 = jnp.float32
BF16 = jnp.bfloat16


def _rms(x, g):
    return x * lax.rsqrt(jnp.mean(x * x, axis=-1, keepdims=True) + EPS) * g


def _proj_kernel(x_ref, g_ref, w_ref, wuk_ref, gkv_ref,
                 qlt_ref, c_ref, cxt_ref, qit_ref, kx_ref, wit_ref, u_ref):
    h = _rms(x_ref[...], g_ref[...]).astype(BF16)
    p = jnp.dot(h, w_ref[...], preferred_element_type=F32)
    tm = p.shape[0]
    o_c = COL_Q
    o_qi = o_c + KV_LATENT
    o_u = o_qi + COL_QI
    o_t = o_u + POOL_WIDTH
    q = p[:, :o_c].astype(BF16)
    for hd in range(N_ATT_HEADS):
        ql = jnp.dot(q[:, hd * ATT_HEAD_DIM:(hd + 1) * ATT_HEAD_DIM], wuk_ref[hd],
                     preferred_element_type=F32)
        qlt_ref[hd] = (ql * (ATT_HEAD_DIM ** -0.5 * LOG2E)).T.astype(BF16)
    c = _rms(p[:, o_c:o_qi], gkv_ref[...])
    c_ref[...] = c.astype(BF16)
    ones_rows = jnp.where(lax.broadcasted_iota(jnp.int32, (ONES_ROWS, tm), 0) == 0, 1.0, 0.0)
    cxt = jnp.concatenate([c.T, ones_rows], axis=0).astype(BF16)
    for tt in range(cxt_ref.shape[0]):
        cxt_ref[tt] = cxt[:, tt * KEY_TILE:(tt + 1) * KEY_TILE]
    qit_ref[...] = p[:, o_qi:o_u].T.astype(BF16)
    u_ref[...] = p[:, o_u:o_t]
    tail = p[:, o_t:o_t + TAIL]
    kx_ref[...] = tail[:, :IDX_HEAD_DIM].astype(BF16)
    wit_ref[...] = tail.T[IDX_HEAD_DIM:IDX_HEAD_DIM + N_IDX_HEADS] * (
        (N_IDX_HEADS ** -0.5) * (IDX_HEAD_DIM ** -0.5))


def _proj(x, g, w_cat, wuk, gkv, tm):
    B, S, D = x.shape
    ncol = w_cat.shape[1]
    grid = (B, S // tm)
    const = lambda *shape: pl.BlockSpec(shape, lambda b, i: (0,) * len(shape))
    return pl.pallas_call(
        _proj_kernel,
        grid=grid,
        in_specs=[
            pl.BlockSpec((None, tm, D), lambda b, i: (b, i, 0)),
            const(1, D),
            const(D, ncol),
            const(N_ATT_HEADS, ATT_HEAD_DIM, KV_LATENT),
            const(1, KV_LATENT),
        ],
        out_specs=[
            pl.BlockSpec((None, N_ATT_HEADS, KV_LATENT, tm), lambda b, i: (b, 0, 0, i)),
            pl.BlockSpec((None, tm, KV_LATENT), lambda b, i: (b, i, 0)),
            pl.BlockSpec((None, tm // KEY_TILE, KV_LATENT + ONES_ROWS, KEY_TILE), lambda b, i: (b, i, 0, 0)),
            pl.BlockSpec((None, COL_QI, tm), lambda b, i: (b, 0, i)),
            pl.BlockSpec((None, tm, IDX_HEAD_DIM), lambda b, i: (b, i, 0)),
            pl.BlockSpec((None, N_IDX_HEADS, tm), lambda b, i: (b, 0, i)),
            pl.BlockSpec((None, tm, POOL_WIDTH), lambda b, i: (b, i, 0)),
        ],
        out_shape=[
            jax.ShapeDtypeStruct((B, N_ATT_HEADS, KV_LATENT, S), BF16),
            jax.ShapeDtypeStruct((B, S, KV_LATENT), BF16),
            jax.ShapeDtypeStruct((B, S // KEY_TILE, KV_LATENT + ONES_ROWS, KEY_TILE), BF16),
            jax.ShapeDtypeStruct((B, COL_QI, S), BF16),
            jax.ShapeDtypeStruct((B, S, IDX_HEAD_DIM), BF16),
            jax.ShapeDtypeStruct((B, N_IDX_HEADS, S), F32),
            jax.ShapeDtypeStruct((B, S, POOL_WIDTH), F32),
        ],
        compiler_params=pltpu.CompilerParams(
            dimension_semantics=("parallel", "parallel"), vmem_limit_bytes=VMEM_LIMIT),
        name="proj",
    )(x, g, w_cat, wuk, gkv)


def _any(mask):
    return jnp.max(jnp.where(mask, 1.0, 0.0)) > 0.0


def _fold_rows(x, op, ways=FOLD_WAYS):
    groups = [x[g * SUBLANES:(g + 1) * SUBLANES] for g in range(x.shape[0] // SUBLANES)]
    parts = groups[:ways]
    for g, grp in enumerate(groups[ways:]):
        parts[g % ways] = op(parts[g % ways], grp)
    while len(parts) > 1:
        parts = [op(parts[i], parts[i + 1]) if i + 1 < len(parts) else parts[i] for i in range(0, len(parts), 2)]
    return parts[0]


def _attn_kernel(qlt_ref, qit_ref, wit_ref, kx_ref, c_ref, cxt_ref, wuvt_ref, a_ref,
                 sm_ref, dm_ref, lg_ref, mx_ref, ox_ref, *, n_top):
    qb = qit_ref.shape[1]
    n_tiles, _, tk = cxt_ref.shape
    dc = c_ref.shape[1]
    j = pl.program_id(1)
    nk = (j + 1) * (qb // tk)

    qpos = j * qb + lax.broadcasted_iota(jnp.int32, (1, qb), 1)
    qchunk = qpos // CHUNK
    n_adm = ((qchunk + 1) * CHUNK).astype(F32)
    k_sel = jnp.minimum(n_adm, float(n_top))
    pos_inf = jnp.full((1, qb), jnp.inf, F32)
    group_zeros = jnp.zeros((SUBLANES, qb), F32)
    group_inf = jnp.full((SUBLANES, qb), jnp.inf, F32)

    def kpos_of(t, n=tk):
        return t * n + lax.broadcasted_iota(jnp.int32, (n, 1), 0)

    n_pairs = (nk + 1) // 2

    def half_of(t):
        return pl.ds(pl.multiple_of((t % 2) * tk, tk), tk)

    @pl.when(nk % 2 == 1)
    def _():
        sm_ref[nk // 2, tk:, :] = jnp.full((tk, qb), -jnp.inf, F32)

    def keys_of(ref, t):
        return ref[pl.ds(pl.multiple_of(t * tk, tk), tk), :]

    def col_sum(x):
        return jnp.sum(x, axis=0, keepdims=True)

    def loop2(n, body, init):
        carry = lax.fori_loop(0, n // 2, lambda i, c: body(2 * i + 1, body(2 * i, c)), init)
        return lax.cond(n % 2 == 1, lambda c: body(n - 1, c), lambda c: c, carry)

    wit = wit_ref[...]

    def score_tile(t, carry):
        mn, mx = carry
        kx = keys_of(kx_ref, t)
        score = None
        for i in range(N_IDX_HEADS):
            rel = jnp.dot(kx, qit_ref[i * IDX_HEAD_DIM:(i + 1) * IDX_HEAD_DIM, :],
                          preferred_element_type=F32)
            term = wit[i:i + 1, :] * jnp.maximum(rel, 0.0)
            score = term if score is None else score + term
        adm = (kpos_of(t) // CHUNK) <= qchunk
        sm = jnp.where(adm, score, -jnp.inf)
        sm_ref[t // 2, half_of(t), :] = sm
        return (jnp.minimum(mn, _fold_rows(jnp.where(adm, score, jnp.inf), jnp.minimum)),
                jnp.maximum(mx, _fold_rows(sm, jnp.maximum)))

    mn, mx = loop2(nk, score_tile, (group_inf, -group_inf))
    q_min = jnp.min(mn, axis=0, keepdims=True)
    q_max = jnp.max(mx, axis=0, keepdims=True)

    state = (q_min, q_max + (jnp.abs(q_max) * (2.0 ** -20) + 1e-30), n_adm, jnp.zeros_like(n_adm))

    def count_ge(mid):
        def body(pr, acc):
            return acc + _fold_rows(jnp.where(sm_ref[pr] >= mid, 1.0, 0.0), jnp.add)
        return col_sum(loop2(n_pairs, body, group_zeros))

    def bisect(n, st):
        lo, hi, clo, chi = st
        for _ in range(n):
            mid = lo + 0.5 * (hi - lo)
            cnt = count_ge(mid)
            ge = cnt >= k_sel
            lo, hi = jnp.where(ge, mid, lo), jnp.where(ge, hi, mid)
            clo, chi = jnp.where(ge, cnt, clo), jnp.where(ge, chi, cnt)
        return lo, hi, clo, chi

    def in_range_ends(st):
        lo, hi = st[0], st[1]

        def body(pr, carry):
            s = sm_ref[pr]
            return (jnp.minimum(carry[0], _fold_rows(jnp.where(s >= lo, s, jnp.inf), jnp.minimum)),
                    jnp.maximum(carry[1], _fold_rows(jnp.where(s < hi, s, -jnp.inf), jnp.maximum)))

        v_lo, v_hi = lax.fori_loop(0, n_pairs, body, (group_inf, -group_inf))
        return jnp.min(v_lo, axis=0, keepdims=True), jnp.max(v_hi, axis=0, keepdims=True)

    def peel(st):
        lo, hi, clo, chi = st
        v_lo, v_hi = in_range_ends(st)
        done = clo == k_sel
        one_short = jnp.logical_and(jnp.logical_not(done), k_sel - chi == 1.0)
        one_over = jnp.logical_and(jnp.logical_not(jnp.logical_or(done, one_short)), clo - k_sel == 1.0)
        settled = jnp.logical_or(jnp.logical_or(done, one_short), jnp.logical_or(one_over, v_lo == v_hi))
        return (jnp.where(one_short, v_hi, lo), jnp.where(one_over, v_lo, pos_inf),
                _any(jnp.logical_not(settled)))

    state = bisect(FIRST_BISECT_STEPS, state)

    def refine(carry):
        _, rounds, st, _, _ = carry
        lo_sel, drop, is_open = peel(st)
        st = lax.cond(is_open, functools.partial(bisect, MORE_BISECT_STEPS), lambda s: s, st)
        return is_open.astype(jnp.int32), rounds + 1, st, lo_sel, drop

    undecided = _any(state[2] != k_sel)
    _, _, state, lo_sel, drop = lax.while_loop(
        lambda c: jnp.logical_and(c[0] > 0, c[1] < MAX_BISECT_ROUNDS), refine,
        (undecided.astype(jnp.int32), jnp.int32(0), state, state[0], pos_inf))

    def select_pair(pr, acc):
        s = sm_ref[pr]
        sel = jnp.logical_and(s >= lo_sel, s != drop)
        dist = jnp.abs(qpos - kpos_of(pr, 2 * tk)).astype(F32)
        dm_ref[pr] = jnp.where(sel, dist, MASKED_DIST)
        return acc + _fold_rows(jnp.where(sel, 1.0, 0.0), jnp.add)

    n_sel = col_sum(lax.fori_loop(0, n_pairs, select_pair, group_zeros))

    wrong = n_sel != k_sel

    @pl.when(_any(wrong))
    def _():
        def unresolved(st):
            v_lo, v_hi = in_range_ends(st)
            open_q = jnp.logical_and(wrong, jnp.logical_and(st[2] != k_sel, v_lo != v_hi))
            return _any(open_q).astype(jnp.int32)

        def tighten(carry):
            _, rounds, st = carry
            st = bisect(TIE_BISECT_STEPS, st)
            return unresolved(st), rounds + 1, st

        _, _, (lo, hi, _, chi) = lax.while_loop(
            lambda c: jnp.logical_and(c[0] > 0, c[1] < MAX_BISECT_ROUNDS), tighten,
            (unresolved(state), jnp.int32(0), state))
        need = k_sel - chi
        key_i = lax.broadcasted_iota(jnp.int32, (tk, tk), 0)
        tri = jnp.where(key_i >= lax.broadcasted_iota(jnp.int32, (tk, tk), 1), 1.0, 0.0).astype(BF16)

        def reselect_tile(t, seen):
            s = sm_ref[t // 2, half_of(t), :]
            in_range = jnp.logical_and(s >= lo, s < hi)
            ones = jnp.where(in_range, 1.0, 0.0)
            rank = seen + jnp.dot(tri, ones.astype(BF16), preferred_element_type=F32)
            sel_x = jnp.logical_or(s >= hi, jnp.logical_and(in_range, rank <= need))
            exact = jnp.where(sel_x, jnp.abs(qpos - kpos_of(t)).astype(F32), MASKED_DIST)
            dm_ref[t // 2, half_of(t), :] = jnp.where(wrong, exact, dm_ref[t // 2, half_of(t), :])
            return seen + col_sum(_fold_rows(ones, jnp.add))

        lax.fori_loop(0, nk, reselect_tile, jnp.zeros((1, qb), F32))

    slopes = [LOG2E * 2.0 ** (-8.0 * (hd + 1) / N_ATT_HEADS) for hd in range(N_ATT_HEADS)]
    for hd in range(N_ATT_HEADS):
        mx_ref[hd] = -group_inf
        ox_ref[hd] = jnp.zeros(ox_ref.shape[1:], F32)

    def top_tile(t, _):
        dm = dm_ref[t // 2, half_of(t), :]
        c = keys_of(c_ref, t)
        for hd in range(N_ATT_HEADS):
            logits = jnp.dot(c, qlt_ref[hd], preferred_element_type=F32) - slopes[hd] * dm
            lg_ref[hd, t] = logits
            mx_ref[hd] = jnp.maximum(mx_ref[hd], _fold_rows(logits, jnp.maximum, ways=1))
        return 0

    loop2(nk, top_tile, 0)
    q_top = [jnp.max(mx_ref[hd], axis=0, keepdims=True) for hd in range(N_ATT_HEADS)]

    def pv_tile(t, _):
        cxt = cxt_ref[t]
        for hd in range(N_ATT_HEADS):
            p = jnp.exp2(lg_ref[hd, t] - q_top[hd]).astype(BF16)
            ox_ref[hd] += jnp.dot(cxt, p, preferred_element_type=F32)
        return 0

    loop2(nk, pv_tile, 0)
    a_heads = []
    for hd in range(N_ATT_HEADS):
        ox = ox_ref[hd]
        o_t = (ox[:dc] / ox[dc:dc + 1]).astype(BF16)
        a_heads.append(jnp.dot(wuvt_ref[hd], o_t, preferred_element_type=F32))
    a_ref[...] = jnp.concatenate(a_heads, axis=0).T.astype(BF16)


def _attn(qlt, qit, wit, kx, c, cxt, wuvt, qb, n_top):
    B, H, dc, S = qlt.shape
    n_tiles, ox_rows, tk = cxt.shape[1:]
    grid = (B, S // qb)
    return pl.pallas_call(
        functools.partial(_attn_kernel, n_top=n_top),
        grid=grid,
        in_specs=[
            pl.BlockSpec((None, H, dc, qb), lambda b, j: (b, 0, 0, j)),
            pl.BlockSpec((None, COL_QI, qb), lambda b, j: (b, 0, j)),
            pl.BlockSpec((None, N_IDX_HEADS, qb), lambda b, j: (b, 0, j)),
            pl.BlockSpec((None, S, IDX_HEAD_DIM), lambda b, j: (b, 0, 0)),
            pl.BlockSpec((None, S, dc), lambda b, j: (b, 0, 0)),
            pl.BlockSpec((None, n_tiles, ox_rows, tk), lambda b, j: (b, 0, 0, 0)),
            pl.BlockSpec((H, ATT_HEAD_DIM, dc), lambda b, j: (0, 0, 0)),
        ],
        out_specs=pl.BlockSpec((None, qb, COL_Q), lambda b, j: (b, j, 0)),
        out_shape=jax.ShapeDtypeStruct((B, S, COL_Q), BF16),
        scratch_shapes=[
            pltpu.VMEM(((n_tiles + 1) // 2, 2 * tk, qb), F32),
            pltpu.VMEM(((n_tiles + 1) // 2, 2 * tk, qb), F32),
            pltpu.VMEM((H, n_tiles, tk, qb), F32),
            pltpu.VMEM((H, SUBLANES, qb), F32),
            pltpu.VMEM((H, ox_rows, qb), F32),
        ],
        compiler_params=pltpu.CompilerParams(
            dimension_semantics=("parallel", "parallel"), vmem_limit_bytes=VMEM_LIMIT),
        name="attn",
    )(qlt, qit, wit, kx, c, cxt, wuvt)


def _mixffn_kernel(x_ref, xh_ref, a_ref, ah_ref, u_ref, uh_ref, wp_ref, ps_ref, wo_ref,
                   g_ref, wup_ref, cw_ref, cb_ref, wdn_ref, gf_ref, o_ref, act_ref, *, final_norm):
    tm, D = x_ref.shape
    n_chunks, _, fc2 = wup_ref.shape
    fc = fc2 // 2
    i = pl.program_id(1)
    rows = CONV_HALO + tm
    first = i == 0

    u_hist = jnp.where(first, 0.0, uh_ref[...])
    ext = jnp.concatenate([u_hist, u_ref[...]], axis=0)
    t = i * tm - CONV_HALO + lax.broadcasted_iota(jnp.int32, (rows, 1), 0)
    mixed = []
    for g, win in enumerate(POOL_WINDOWS):
        e = ext[:, g * POOL_GROUP:(g + 1) * POOL_GROUP]
        acc = e
        span = 1
        while span < win:
            acc = acc + pltpu.roll(acc, span, axis=0)
            span *= 2
        count = jnp.clip(t + 1, 1, win).astype(F32)
        pooled = acc[U_HALO - CONV_HALO:] / count - e[U_HALO - CONV_HALO:]
        mixed.append(jnp.dot(pooled.astype(BF16), wp_ref[g], preferred_element_type=F32))
    b = (jnp.concatenate(mixed, axis=1) * ps_ref[...]).astype(BF16)
    a = jnp.concatenate([ah_ref[A_HALO - CONV_HALO:], a_ref[...]], axis=0)
    xe = jnp.concatenate([xh_ref[...], x_ref[...]], axis=0)
    xe = xe + jnp.dot(jnp.concatenate([a, b], axis=1), wo_ref[...], preferred_element_type=F32)
    x = xe[CONV_HALO:]

    h = _rms(xe, g_ref[...])
    row = lax.broadcasted_iota(jnp.int32, (rows, 1), 0)
    h = jnp.where(jnp.logical_and(first, row < CONV_HALO), 0.0, h).astype(BF16)

    for k in range(n_chunks):
        up = jnp.dot(h, wup_ref[k], preferred_element_type=F32)
        cw = cw_ref[k]
        conv = cb_ref[k] + cw[2:3] * up
        for jj in range(CONV_WIDTH - 1):
            conv = conv + cw[jj:jj + 1] * pltpu.roll(up, CONV_WIDTH - 1 - jj, axis=0)
        conv = conv[CONV_HALO:]
        gate = conv[:, :fc]
        act = gate * (1.0 / (1.0 + jnp.exp(-gate))) * conv[:, fc:]
        act_ref[:, k * fc:(k + 1) * fc] = act.astype(BF16)

    y = x + jnp.dot(act_ref[...], wdn_ref[...], preferred_element_type=F32)
    if final_norm:
        y = _rms(y, gf_ref[...])
    o_ref[...] = y


def _mixffn(x, a, u, w_pool, pool_scale, w_o, g, wup, cw, cb, wdn, g_final, tm, final_norm):
    B, S, D = x.shape
    grid = (B, S // tm)
    hist = lambda rows: (lambda b, i: (b, jnp.maximum(i * (tm // rows) - 1, 0), 0))
    tile = lambda b, i: (b, i, 0)
    const = lambda *shape: pl.BlockSpec(shape, lambda b, i: (0,) * len(shape), pipeline_mode=pl.Buffered(1))
    return pl.pallas_call(
        functools.partial(_mixffn_kernel, final_norm=final_norm),
        grid=grid,
        in_specs=[
            pl.BlockSpec((None, tm, D), tile),
            pl.BlockSpec((None, CONV_HALO, D), hist(CONV_HALO)),
            pl.BlockSpec((None, tm, COL_Q), tile),
            pl.BlockSpec((None, A_HALO, COL_Q), hist(A_HALO)),
            pl.BlockSpec((None, tm, POOL_WIDTH), tile),
            pl.BlockSpec((None, U_HALO, POOL_WIDTH), hist(U_HALO)),
            const(*w_pool.shape),
            const(1, POOL_WIDTH),
            const(*w_o.shape),
            const(1, D),
            const(*wup.shape),
            const(*cw.shape),
            const(*cb.shape),
            const(*wdn.shape),
            const(1, D),
        ],
        out_specs=pl.BlockSpec((None, tm, D), tile),
        out_shape=jax.ShapeDtypeStruct((B, S, D), F32),
        scratch_shapes=[pltpu.VMEM((tm, wdn.shape[0]), BF16)],
        compiler_params=pltpu.CompilerParams(
            dimension_semantics=("parallel", "parallel"), vmem_limit_bytes=VMEM_LIMIT),
        name="mixffn",
    )(x, x, a, a, u, u, w_pool, pool_scale, w_o, g, wup, cw, cb, wdn, g_final)


def _ffn_chunk(d_ff):
    for fc in (256, 128):
        if d_ff % fc == 0:
            return fc
    raise ValueError(f"unsupported FFN width {d_ff}")


def kernel(x, g_mix, w_in, g_kv, w_uk, w_uv, w_pool, pool_scale, w_o, g_ffn, w_up, conv_w, conv_b, w_down, g_final):
    B, S, D = x.shape
    depth = g_mix.shape[0]
    d_ff = w_down.shape[1]
    fc = _ffn_chunk(d_ff)
    n_chunks = d_ff // fc
    n_top = min(TOPK_MAX, S // 4)
    tm = min(512, S)
    qb = min(256, S)
    assert S % tm == 0 and S % qb == 0 and qb % KEY_TILE == 0 and tm % KEY_TILE == 0 and KEY_TILE % CHUNK == 0
    assert w_in.shape[2] == COL_Q + KV_LATENT + COL_QI + IDX_HEAD_DIM + N_IDX_HEADS + POOL_WIDTH

    s_q = COL_Q
    s_c = s_q + KV_LATENT
    s_qi = s_c + COL_QI
    s_ki = s_qi + IDX_HEAD_DIM
    s_wi = s_ki + N_IDX_HEADS
    row = lambda v: v.reshape(1, -1).astype(F32)

    for l in range(depth):
        w = w_in[l]
        w_cat = jnp.concatenate(
            [w[:, :s_q], w[:, s_q:s_c], w[:, s_c:s_qi], w[:, s_wi:], w[:, s_qi:s_ki], w[:, s_ki:s_wi],
             jnp.zeros((D, TAIL - IDX_HEAD_DIM - N_IDX_HEADS), w.dtype)], axis=1).astype(BF16)
        wuvt = jnp.swapaxes(w_uv[l], 1, 2).astype(BF16)
        pair = lambda m: jnp.concatenate(
            [m[..., :d_ff].reshape(m.shape[:-1] + (n_chunks, fc)),
             m[..., d_ff:].reshape(m.shape[:-1] + (n_chunks, fc))], axis=-1)
        wup = jnp.moveaxis(pair(w_up[l]), 1, 0).astype(BF16)
        cw = jnp.moveaxis(pair(conv_w[l]), 1, 0).astype(F32)
        cb = pair(conv_b[l]).reshape(n_chunks, 1, 2 * fc).astype(F32)
        wdn = w_down[l].astype(BF16)

        qlt, c, cxt, qit, kx, wit, u = _proj(x, row(g_mix[l]), w_cat, w_uk[l].astype(BF16), row(g_kv[l]), tm)
        a = _attn(qlt, qit, wit, kx, c, cxt, wuvt, qb, n_top)
        x = _mixffn(x, a, u, w_pool[l].astype(BF16), row(pool_scale[l]), w_o[l].astype(BF16),
                    row(g_ffn[l]), wup, cw, cb, wdn, row(g_final), tm, final_norm=(l == depth - 1))
    return x
```

```python
import functools

import jax
import jax.numpy as jnp
from jax import lax
from jax.experimental import pallas as pl
from jax.experimental.pallas import tpu as pltpu

CHUNK = 64
N_ATT_HEADS = 8
ATT_HEAD_DIM = 64
KV_LATENT = 128
N_IDX_HEADS = 8
IDX_HEAD_DIM = 64
TOPK_MAX = 256
POOL_WINDOWS = (2, 4, 8, 16)
POOL_GROUP = 128
CONV_WIDTH = 3
EPS = 1e-6

COL_Q = N_ATT_HEADS * ATT_HEAD_DIM
COL_QI = N_IDX_HEADS * IDX_HEAD_DIM
POOL_WIDTH = POOL_GROUP * len(POOL_WINDOWS)
TAIL = 128
U_HALO = 32
A_HALO = 16
CONV_HALO = 8
MASKED_DIST = 1e30
FIRST_BISECT_STEPS = 14
MORE_BISECT_STEPS = 2
TIE_BISECT_STEPS = 8
MAX_BISECT_ROUNDS = 100
ONES_ROWS = 16
SUBLANES = 8
FOLD_WAYS = 4
KEY_TILE = 256
LOG2E = 1.4426950408889634
VMEM_LIMIT = 56 * 1024 * 1024

F32 = jnp.float32
BF16 = jnp.bfloat16


def _rms(x, g):
    return x * lax.rsqrt(jnp.mean(x * x, axis=-1, keepdims=True) + EPS) * g


def _proj_kernel(x_ref, g_ref, w_ref, wuk_ref, gkv_ref,
                 qlt_ref, c_ref, cxt_ref, qit_ref, kx_ref, wit_ref, u_ref):
    h = _rms(x_ref[...], g_ref[...]).astype(BF16)
    p = jnp.dot(h, w_ref[...], preferred_element_type=F32)
    tm = p.shape[0]
    o_c = COL_Q
    o_qi = o_c + KV_LATENT
    o_u = o_qi + COL_QI
    o_t = o_u + POOL_WIDTH
    q = p[:, :o_c].astype(BF16)
    for hd in range(N_ATT_HEADS):
        ql = jnp.dot(q[:, hd * ATT_HEAD_DIM:(hd + 1) * ATT_HEAD_DIM], wuk_ref[hd],
                     preferred_element_type=F32)
        qlt_ref[hd] = (ql * (ATT_HEAD_DIM ** -0.5 * LOG2E)).T.astype(BF16)
    c = _rms(p[:, o_c:o_qi], gkv_ref[...])
    c_ref[...] = c.astype(BF16)
    ones_rows = jnp.where(lax.broadcasted_iota(jnp.int32, (ONES_ROWS, tm), 0) == 0, 1.0, 0.0)
    cxt = jnp.concatenate([c.T, ones_rows], axis=0).astype(BF16)
    for tt in range(cxt_ref.shape[0]):
        cxt_ref[tt] = cxt[:, tt * KEY_TILE:(tt + 1) * KEY_TILE]
    qit_ref[...] = p[:, o_qi:o_u].T.astype(BF16)
    u_ref[...] = p[:, o_u:o_t]
    tail = p[:, o_t:o_t + TAIL]
    kx_ref[...] = tail[:, :IDX_HEAD_DIM].astype(BF16)
    wit_ref[...] = tail.T[IDX_HEAD_DIM:IDX_HEAD_DIM + N_IDX_HEADS] * (
        (N_IDX_HEADS ** -0.5) * (IDX_HEAD_DIM ** -0.5))


def _proj(x, g, w_cat, wuk, gkv, tm):
    B, S, D = x.shape
    ncol = w_cat.shape[1]
    grid = (B, S // tm)
    const = lambda *shape: pl.BlockSpec(shape, lambda b, i: (0,) * len(shape))
    return pl.pallas_call(
        _proj_kernel,
        grid=grid,
        in_specs=[
            pl.BlockSpec((None, tm, D), lambda b, i: (b, i, 0)),
            const(1, D),
            const(D, ncol),
            const(N_ATT_HEADS, ATT_HEAD_DIM, KV_LATENT),
            const(1, KV_LATENT),
        ],
        out_specs=[
            pl.BlockSpec((None, N_ATT_HEADS, KV_LATENT, tm), lambda b, i: (b, 0, 0, i)),
            pl.BlockSpec((None, tm, KV_LATENT), lambda b, i: (b, i, 0)),
            pl.BlockSpec((None, tm // KEY_TILE, KV_LATENT + ONES_ROWS, KEY_TILE), lambda b, i: (b, i, 0, 0)),
            pl.BlockSpec((None, COL_QI, tm), lambda b, i: (b, 0, i)),
            pl.BlockSpec((None, tm, IDX_HEAD_DIM), lambda b, i: (b, i, 0)),
            pl.BlockSpec((None, N_IDX_HEADS, tm), lambda b, i: (b, 0, i)),
            pl.BlockSpec((None, tm, POOL_WIDTH), lambda b, i: (b, i, 0)),
        ],
        out_shape=[
            jax.ShapeDtypeStruct((B, N_ATT_HEADS, KV_LATENT, S), BF16),
            jax.ShapeDtypeStruct((B, S, KV_LATENT), BF16),
            jax.ShapeDtypeStruct((B, S // KEY_TILE, KV_LATENT + ONES_ROWS, KEY_TILE), BF16),
            jax.ShapeDtypeStruct((B, COL_QI, S), BF16),
            jax.ShapeDtypeStruct((B, S, IDX_HEAD_DIM), BF16),
            jax.ShapeDtypeStruct((B, N_IDX_HEADS, S), F32),
            jax.ShapeDtypeStruct((B, S, POOL_WIDTH), F32),
        ],
        compiler_params=pltpu.CompilerParams(
            dimension_semantics=("parallel", "parallel"), vmem_limit_bytes=VMEM_LIMIT),
        name="proj",
    )(x, g, w_cat, wuk, gkv)


def _any(mask):
    return jnp.max(jnp.where(mask, 1.0, 0.0)) > 0.0


def _fold_rows(x, op, ways=FOLD_WAYS):
    groups = [x[g * SUBLANES:(g + 1) * SUBLANES] for g in range(x.shape[0] // SUBLANES)]
    parts = groups[:ways]
    for g, grp in enumerate(groups[ways:]):
        parts[g % ways] = op(parts[g % ways], grp)
    while len(parts) > 1:
        parts = [op(parts[i], parts[i + 1]) if i + 1 < len(parts) else parts[i] for i in range(0, len(parts), 2)]
    return parts[0]


def _attn_kernel(qlt_ref, qit_ref, wit_ref, kx_ref, c_ref, cxt_ref, wuvt_ref, a_ref,
                 sm_ref, dm_ref, lg_ref, mx_ref, ox_ref, *, n_top):
    qb = qit_ref.shape[1]
    n_tiles, _, tk = cxt_ref.shape
    dc = c_ref.shape[1]
    j = pl.program_id(1)
    nk = (j + 1) * (qb // tk)

    qpos = j * qb + lax.broadcasted_iota(jnp.int32, (1, qb), 1)
    qchunk = qpos // CHUNK
    n_adm = ((qchunk + 1) * CHUNK).astype(F32)
    k_sel = jnp.minimum(n_adm, float(n_top))
    pos_inf = jnp.full((1, qb), jnp.inf, F32)
    group_zeros = jnp.zeros((SUBLANES, qb), F32)
    group_inf = jnp.full((SUBLANES, qb), jnp.inf, F32)

    def kpos_of(t, n=tk):
        return t * n + lax.broadcasted_iota(jnp.int32, (n, 1), 0)

    n_pairs = (nk + 1) // 2

    def half_of(t):
        return pl.ds(pl.multiple_of((t % 2) * tk, tk), tk)

    @pl.when(nk % 2 == 1)
    def _():
        sm_ref[nk // 2, tk:, :] = jnp.full((tk, qb), -jnp.inf, F32)

    def keys_of(ref, t):
        return ref[pl.ds(pl.multiple_of(t * tk, tk), tk), :]

    def col_sum(x):
        return jnp.sum(x, axis=0, keepdims=True)

    def loop2(n, body, init):
        carry = lax.fori_loop(0, n // 2, lambda i, c: body(2 * i + 1, body(2 * i, c)), init)
        return lax.cond(n % 2 == 1, lambda c: body(n - 1, c), lambda c: c, carry)

    wit = wit_ref[...]

    def score_tile(t, carry, own_chunks):
        mn, mx = carry
        kx = keys_of(kx_ref, t)
        score = None
        for i in range(N_IDX_HEADS):
            rel = jnp.dot(kx, qit_ref[i * IDX_HEAD_DIM:(i + 1) * IDX_HEAD_DIM, :],
                          preferred_element_type=F32)
            term = wit[i:i + 1, :] * jnp.maximum(rel, 0.0)
            score = term if score is None else score + term
        if own_chunks:
            adm = (kpos_of(t) // CHUNK) <= qchunk
            lowest, score = jnp.where(adm, score, jnp.inf), jnp.where(adm, score, -jnp.inf)
        else:
            lowest = score
        sm_ref[t // 2, half_of(t), :] = score
        return (jnp.minimum(mn, _fold_rows(lowest, jnp.minimum)), jnp.maximum(mx, _fold_rows(score, jnp.maximum)))

    n_own = qb // tk
    carry = loop2(nk - n_own, functools.partial(score_tile, own_chunks=False), (group_inf, -group_inf))
    for own in range(n_own):
        carry = score_tile(nk - n_own + own, carry, own_chunks=True)
    mn, mx = carry
    q_min = jnp.min(mn, axis=0, keepdims=True)
    q_max = jnp.max(mx, axis=0, keepdims=True)

    state = (q_min, q_max + (jnp.abs(q_max) * (2.0 ** -20) + 1e-30), n_adm, jnp.zeros_like(n_adm))

    def count_ge(mid):
        def body(pr, acc):
            return acc + _fold_rows(jnp.where(sm_ref[pr] >= mid, 1.0, 0.0), jnp.add)
        return col_sum(loop2(n_pairs, body, group_zeros))

    def bisect(n, st):
        lo, hi, clo, chi = st
        for _ in range(n):
            mid = lo + 0.5 * (hi - lo)
            cnt = count_ge(mid)
            ge = cnt >= k_sel
            lo, hi = jnp.where(ge, mid, lo), jnp.where(ge, hi, mid)
            clo, chi = jnp.where(ge, cnt, clo), jnp.where(ge, chi, cnt)
        return lo, hi, clo, chi

    def in_range_ends(st):
        lo, hi = st[0], st[1]

        def body(pr, carry):
            s = sm_ref[pr]
            return (jnp.minimum(carry[0], _fold_rows(jnp.where(s >= lo, s, jnp.inf), jnp.minimum)),
                    jnp.maximum(carry[1], _fold_rows(jnp.where(s < hi, s, -jnp.inf), jnp.maximum)))

        v_lo, v_hi = loop2(n_pairs, body, (group_inf, -group_inf))
        return jnp.min(v_lo, axis=0, keepdims=True), jnp.max(v_hi, axis=0, keepdims=True)

    def peel(st):
        lo, hi, clo, chi = st
        v_lo, v_hi = in_range_ends(st)
        done = clo == k_sel
        one_short = jnp.logical_and(jnp.logical_not(done), k_sel - chi == 1.0)
        one_over = jnp.logical_and(jnp.logical_not(jnp.logical_or(done, one_short)), clo - k_sel == 1.0)
        settled = jnp.logical_or(jnp.logical_or(done, one_short), jnp.logical_or(one_over, v_lo == v_hi))
        return (jnp.where(one_short, v_hi, lo), jnp.where(one_over, v_lo, pos_inf),
                _any(jnp.logical_not(settled)))

    state = lax.cond(nk * tk > n_top, functools.partial(bisect, FIRST_BISECT_STEPS), lambda st: st, state)

    def refine(carry):
        _, rounds, st, _, _ = carry
        lo_sel, drop, is_open = peel(st)
        st = lax.cond(is_open, functools.partial(bisect, MORE_BISECT_STEPS), lambda s: s, st)
        return is_open.astype(jnp.int32), rounds + 1, st, lo_sel, drop

    undecided = _any(state[2] != k_sel)
    _, _, state, lo_sel, drop = lax.while_loop(
        lambda c: jnp.logical_and(c[0] > 0, c[1] < MAX_BISECT_ROUNDS), refine,
        (undecided.astype(jnp.int32), jnp.int32(0), state, state[0], pos_inf))

    def select_pair(pr, acc):
        s = sm_ref[pr]
        sel = jnp.logical_and(s >= lo_sel, s != drop)
        dist = jnp.abs(qpos - kpos_of(pr, 2 * tk)).astype(F32)
        dm_ref[pr] = jnp.where(sel, dist, MASKED_DIST)
        return acc + _fold_rows(jnp.where(sel, 1.0, 0.0), jnp.add)

    n_sel = col_sum(loop2(n_pairs, select_pair, group_zeros))

    wrong = n_sel != k_sel

    @pl.when(_any(wrong))
    def _():
        def unresolved(st):
            v_lo, v_hi = in_range_ends(st)
            open_q = jnp.logical_and(wrong, jnp.logical_and(st[2] != k_sel, v_lo != v_hi))
            return _any(open_q).astype(jnp.int32)

        def tighten(carry):
            _, rounds, st = carry
            st = bisect(TIE_BISECT_STEPS, st)
            return unresolved(st), rounds + 1, st

        _, _, (lo, hi, _, chi) = lax.while_loop(
            lambda c: jnp.logical_and(c[0] > 0, c[1] < MAX_BISECT_ROUNDS), tighten,
            (unresolved(state), jnp.int32(0), state))
        need = k_sel - chi
        key_i = lax.broadcasted_iota(jnp.int32, (tk, tk), 0)
        tri = jnp.where(key_i >= lax.broadcasted_iota(jnp.int32, (tk, tk), 1), 1.0, 0.0).astype(BF16)

        def reselect_tile(t, seen):
            s = sm_ref[t // 2, half_of(t), :]
            in_range = jnp.logical_and(s >= lo, s < hi)
            ones = jnp.where(in_range, 1.0, 0.0)
            rank = seen + jnp.dot(tri, ones.astype(BF16), preferred_element_type=F32)
            sel_x = jnp.logical_or(s >= hi, jnp.logical_and(in_range, rank <= need))
            exact = jnp.where(sel_x, jnp.abs(qpos - kpos_of(t)).astype(F32), MASKED_DIST)
            dm_ref[t // 2, half_of(t), :] = jnp.where(wrong, exact, dm_ref[t // 2, half_of(t), :])
            return seen + col_sum(_fold_rows(ones, jnp.add))

        loop2(nk, reselect_tile, jnp.zeros((1, qb), F32))

    slopes = [LOG2E * 2.0 ** (-8.0 * (hd + 1) / N_ATT_HEADS) for hd in range(N_ATT_HEADS)]
    for hd in range(N_ATT_HEADS):
        mx_ref[hd] = -group_inf
        ox_ref[hd] = jnp.zeros(ox_ref.shape[1:], F32)

    def top_tile(t, _):
        dm = dm_ref[t // 2, half_of(t), :]
        c = keys_of(c_ref, t)
        for hd in range(N_ATT_HEADS):
            logits = jnp.dot(c, qlt_ref[hd], preferred_element_type=F32) - slopes[hd] * dm
            lg_ref[hd, t] = logits
            mx_ref[hd] = jnp.maximum(mx_ref[hd], _fold_rows(logits, jnp.maximum, ways=1))
        return 0

    loop2(nk, top_tile, 0)
    q_top = [jnp.max(mx_ref[hd], axis=0, keepdims=True) for hd in range(N_ATT_HEADS)]

    def pv_tile(t, _):
        cxt = cxt_ref[t]
        for hd in range(N_ATT_HEADS):
            p = jnp.exp2(lg_ref[hd, t] - q_top[hd]).astype(BF16)
            ox_ref[hd] += jnp.dot(cxt, p, preferred_element_type=F32)
        return 0

    loop2(nk, pv_tile, 0)
    a_heads = []
    for hd in range(N_ATT_HEADS):
        ox = ox_ref[hd]
        o_t = (ox[:dc] / ox[dc:dc + 1]).astype(BF16)
        a_heads.append(jnp.dot(wuvt_ref[hd], o_t, preferred_element_type=F32))
    a_ref[...] = jnp.concatenate(a_heads, axis=0).T.astype(BF16)


def _attn(qlt, qit, wit, kx, c, cxt, wuvt, qb, n_top):
    B, H, dc, S = qlt.shape
    n_tiles, ox_rows, tk = cxt.shape[1:]
    grid = (B, S // qb)
    return pl.pallas_call(
        functools.partial(_attn_kernel, n_top=n_top),
        grid=grid,
        in_specs=[
            pl.BlockSpec((None, H, dc, qb), lambda b, j: (b, 0, 0, j)),
            pl.BlockSpec((None, COL_QI, qb), lambda b, j: (b, 0, j)),
            pl.BlockSpec((None, N_IDX_HEADS, qb), lambda b, j: (b, 0, j)),
            pl.BlockSpec((None, S, IDX_HEAD_DIM), lambda b, j: (b, 0, 0)),
            pl.BlockSpec((None, S, dc), lambda b, j: (b, 0, 0)),
            pl.BlockSpec((None, n_tiles, ox_rows, tk), lambda b, j: (b, 0, 0, 0)),
            pl.BlockSpec((H, ATT_HEAD_DIM, dc), lambda b, j: (0, 0, 0)),
        ],
        out_specs=pl.BlockSpec((None, qb, COL_Q), lambda b, j: (b, j, 0)),
        out_shape=jax.ShapeDtypeStruct((B, S, COL_Q), BF16),
        scratch_shapes=[
            pltpu.VMEM(((n_tiles + 1) // 2, 2 * tk, qb), F32),
            pltpu.VMEM(((n_tiles + 1) // 2, 2 * tk, qb), F32),
            pltpu.VMEM((H, n_tiles, tk, qb), F32),
            pltpu.VMEM((H, SUBLANES, qb), F32),
            pltpu.VMEM((H, ox_rows, qb), F32),
        ],
        compiler_params=pltpu.CompilerParams(
            dimension_semantics=("parallel", "parallel"), vmem_limit_bytes=VMEM_LIMIT),
        name="attn",
    )(qlt, qit, wit, kx, c, cxt, wuvt)


def _mixffn_kernel(x_ref, xh_ref, a_ref, ah_ref, u_ref, uh_ref, wp_ref, ps_ref, wo_ref,
                   g_ref, wup_ref, cw_ref, cb_ref, wdn_ref, gf_ref, o_ref, act_ref, *, final_norm):
    tm, D = x_ref.shape
    n_chunks, _, fc2 = wup_ref.shape
    fc = fc2 // 2
    i = pl.program_id(1)
    rows = CONV_HALO + tm
    first = i == 0

    u_hist = jnp.where(first, 0.0, uh_ref[...])
    ext = jnp.concatenate([u_hist, u_ref[...]], axis=0)
    t = i * tm - CONV_HALO + lax.broadcasted_iota(jnp.int32, (rows, 1), 0)
    mixed = []
    for g, win in enumerate(POOL_WINDOWS):
        e = ext[:, g * POOL_GROUP:(g + 1) * POOL_GROUP]
        acc = e
        span = 1
        while span < win:
            acc = acc + pltpu.roll(acc, span, axis=0)
            span *= 2
        count = jnp.clip(t + 1, 1, win).astype(F32)
        pooled = acc[U_HALO - CONV_HALO:] / count - e[U_HALO - CONV_HALO:]
        mixed.append(jnp.dot(pooled.astype(BF16), wp_ref[g], preferred_element_type=F32))
    b = (jnp.concatenate(mixed, axis=1) * ps_ref[...]).astype(BF16)
    a = jnp.concatenate([ah_ref[A_HALO - CONV_HALO:], a_ref[...]], axis=0)
    xe = jnp.concatenate([xh_ref[...], x_ref[...]], axis=0)
    xe = xe + jnp.dot(jnp.concatenate([a, b], axis=1), wo_ref[...], preferred_element_type=F32)
    x = xe[CONV_HALO:]

    h = _rms(xe, g_ref[...])
    row = lax.broadcasted_iota(jnp.int32, (rows, 1), 0)
    h = jnp.where(jnp.logical_and(first, row < CONV_HALO), 0.0, h).astype(BF16)

    for k in range(n_chunks):
        up = jnp.dot(h, wup_ref[k], preferred_element_type=F32)
        cw = cw_ref[k]
        conv = cb_ref[k] + cw[2:3] * up
        for jj in range(CONV_WIDTH - 1):
            conv = conv + cw[jj:jj + 1] * pltpu.roll(up, CONV_WIDTH - 1 - jj, axis=0)
        conv = conv[CONV_HALO:]
        gate = conv[:, :fc]
        act = gate * (1.0 / (1.0 + jnp.exp(-gate))) * conv[:, fc:]
        act_ref[:, k * fc:(k + 1) * fc] = act.astype(BF16)

    y = x + jnp.dot(act_ref[...], wdn_ref[...], preferred_element_type=F32)
    if final_norm:
        y = _rms(y, gf_ref[...])
    o_ref[...] = y


def _mixffn(x, a, u, w_pool, pool_scale, w_o, g, wup, cw, cb, wdn, g_final, tm, final_norm):
    B, S, D = x.shape
    grid = (B, S // tm)
    hist = lambda rows: (lambda b, i: (b, jnp.maximum(i * (tm // rows) - 1, 0), 0))
    tile = lambda b, i: (b, i, 0)
    const = lambda *shape: pl.BlockSpec(shape, lambda b, i: (0,) * len(shape), pipeline_mode=pl.Buffered(1))
    return pl.pallas_call(
        functools.partial(_mixffn_kernel, final_norm=final_norm),
        grid=grid,
        in_specs=[
            pl.BlockSpec((None, tm, D), tile),
            pl.BlockSpec((None, CONV_HALO, D), hist(CONV_HALO)),
            pl.BlockSpec((None, tm, COL_Q), tile),
            pl.BlockSpec((None, A_HALO, COL_Q), hist(A_HALO)),
            pl.BlockSpec((None, tm, POOL_WIDTH), tile),
            pl.BlockSpec((None, U_HALO, POOL_WIDTH), hist(U_HALO)),
            const(*w_pool.shape),
            const(1, POOL_WIDTH),
            const(*w_o.shape),
            const(1, D),
            const(*wup.shape),
            const(*cw.shape),
            const(*cb.shape),
            const(*wdn.shape),
            const(1, D),
        ],
        out_specs=pl.BlockSpec((None, tm, D), tile),
        out_shape=jax.ShapeDtypeStruct((B, S, D), F32),
        scratch_shapes=[pltpu.VMEM((tm, wdn.shape[0]), BF16)],
        compiler_params=pltpu.CompilerParams(
            dimension_semantics=("parallel", "parallel"), vmem_limit_bytes=VMEM_LIMIT),
        name="mixffn",
    )(x, x, a, a, u, u, w_pool, pool_scale, w_o, g, wup, cw, cb, wdn, g_final)


def _ffn_chunk(d_ff):
    for fc in (256, 128):
        if d_ff % fc == 0:
            return fc
    raise ValueError(f"unsupported FFN width {d_ff}")


def kernel(x, g_mix, w_in, g_kv, w_uk, w_uv, w_pool, pool_scale, w_o, g_ffn, w_up, conv_w, conv_b, w_down, g_final):
    B, S, D = x.shape
    depth = g_mix.shape[0]
    d_ff = w_down.shape[1]
    fc = _ffn_chunk(d_ff)
    n_chunks = d_ff // fc
    n_top = min(TOPK_MAX, S // 4)
    tm = min(512, S)
    qb = min(256, S)
    assert S % tm == 0 and S % qb == 0 and qb % KEY_TILE == 0 and tm % KEY_TILE == 0 and KEY_TILE % CHUNK == 0
    assert w_in.shape[2] == COL_Q + KV_LATENT + COL_QI + IDX_HEAD_DIM + N_IDX_HEADS + POOL_WIDTH

    s_q = COL_Q
    s_c = s_q + KV_LATENT
    s_qi = s_c + COL_QI
    s_ki = s_qi + IDX_HEAD_DIM
    s_wi = s_ki + N_IDX_HEADS
    row = lambda v: v.reshape(1, -1).astype(F32)

    for l in range(depth):
        w = w_in[l]
        w_cat = jnp.concatenate(
            [w[:, :s_q], w[:, s_q:s_c], w[:, s_c:s_qi], w[:, s_wi:], w[:, s_qi:s_ki], w[:, s_ki:s_wi],
             jnp.zeros((D, TAIL - IDX_HEAD_DIM - N_IDX_HEADS), w.dtype)], axis=1).astype(BF16)
        wuvt = jnp.swapaxes(w_uv[l], 1, 2).astype(BF16)
        pair = lambda m: jnp.concatenate(
            [m[..., :d_ff].reshape(m.shape[:-1] + (n_chunks, fc)),
             m[..., d_ff:].reshape(m.shape[:-1] + (n_chunks, fc))], axis=-1)
        wup = jnp.moveaxis(pair(w_up[l]), 1, 0).astype(BF16)
        cw = jnp.moveaxis(pair(conv_w[l]), 1, 0).astype(F32)
        cb = pair(conv_b[l]).reshape(n_chunks, 1, 2 * fc).astype(F32)
        wdn = w_down[l].astype(BF16)

        qlt, c, cxt, qit, kx, wit, u = _proj(x, row(g_mix[l]), w_cat, w_uk[l].astype(BF16), row(g_kv[l]), tm)
        a = _attn(qlt, qit, wit, kx, c, cxt, wuvt, qb, n_top)
        x = _mixffn(x, a, u, w_pool[l].astype(BF16), row(pool_scale[l]), w_o[l].astype(BF16),
                    row(g_ffn[l]), wup, cw, cb, wdn, row(g_final), tm, final_norm=(l == depth - 1))
    return x
```

```python
import functools

import jax
import jax.numpy as jnp
from jax import lax
from jax.experimental import pallas as pl
from jax.experimental.pallas import tpu as pltpu

CHUNK = 64
N_ATT_HEADS = 8
ATT_HEAD_DIM = 64
KV_LATENT = 128
N_IDX_HEADS = 8
IDX_HEAD_DIM = 64
TOPK_MAX = 256
POOL_WINDOWS = (2, 4, 8, 16)
POOL_GROUP = 128
CONV_WIDTH = 3
EPS = 1e-6

COL_Q = N_ATT_HEADS * ATT_HEAD_DIM
COL_QI = N_IDX_HEADS * IDX_HEAD_DIM
POOL_WIDTH = POOL_GROUP * len(POOL_WINDOWS)
TAIL = 128
U_HALO = 32
A_HALO = 16
CONV_HALO = 8
MASKED_DIST = 1e30
FIRST_BISECT_STEPS = 14
MORE_BISECT_STEPS = 2
TIE_BISECT_STEPS = 8
MAX_BISECT_ROUNDS = 100
ONES_ROWS = 16
SUBLANES = 8
FOLD_WAYS = 4
KEY_TILE = 256
LOG2E = 1.4426950408889634
VMEM_LIMIT = 56 * 1024 * 1024

F32 = jnp.float32
BF16 = jnp.bfloat16


def _rms(x, g):
    return x * lax.rsqrt(jnp.mean(x * x, axis=-1, keepdims=True) + EPS) * g


def _proj_kernel(x_ref, g_ref, w_ref, wuk_ref, gkv_ref,
                 qlt_ref, c_ref, cxt_ref, qit_ref, kx_ref, wit_ref, u_ref):
    h = _rms(x_ref[...], g_ref[...]).astype(BF16)
    p = jnp.dot(h, w_ref[...], preferred_element_type=F32)
    tm = p.shape[0]
    o_c = COL_Q
    o_qi = o_c + KV_LATENT
    o_u = o_qi + COL_QI
    o_t = o_u + POOL_WIDTH
    q = p[:, :o_c].astype(BF16)
    for hd in range(N_ATT_HEADS):
        ql = jnp.dot(q[:, hd * ATT_HEAD_DIM:(hd + 1) * ATT_HEAD_DIM], wuk_ref[hd],
                     preferred_element_type=F32)
        qlt_ref[hd] = (ql * (ATT_HEAD_DIM ** -0.5 * LOG2E)).T.astype(BF16)
    c = _rms(p[:, o_c:o_qi], gkv_ref[...])
    c_ref[...] = c.astype(BF16)
    ones_rows = jnp.where(lax.broadcasted_iota(jnp.int32, (ONES_ROWS, tm), 0) == 0, 1.0, 0.0)
    cxt = jnp.concatenate([c.T, ones_rows], axis=0).astype(BF16)
    for tt in range(cxt_ref.shape[0]):
        cxt_ref[tt] = cxt[:, tt * KEY_TILE:(tt + 1) * KEY_TILE]
    qit_ref[...] = p[:, o_qi:o_u].T.astype(BF16)
    u_ref[...] = p[:, o_u:o_t]
    tail = p[:, o_t:o_t + TAIL]
    kx_ref[...] = tail[:, :IDX_HEAD_DIM].astype(BF16)
    wit_ref[...] = tail.T[IDX_HEAD_DIM:IDX_HEAD_DIM + N_IDX_HEADS] * (
        (N_IDX_HEADS ** -0.5) * (IDX_HEAD_DIM ** -0.5))


def _proj(x, g, w_cat, wuk, gkv, tm):
    B, S, D = x.shape
    ncol = w_cat.shape[1]
    grid = (B, S // tm)
    const = lambda *shape: pl.BlockSpec(shape, lambda b, i: (0,) * len(shape))
    return pl.pallas_call(
        _proj_kernel,
        grid=grid,
        in_specs=[
            pl.BlockSpec((None, tm, D), lambda b, i: (b, i, 0)),
            const(1, D),
            const(D, ncol),
            const(N_ATT_HEADS, ATT_HEAD_DIM, KV_LATENT),
            const(1, KV_LATENT),
        ],
        out_specs=[
            pl.BlockSpec((None, N_ATT_HEADS, KV_LATENT, tm), lambda b, i: (b, 0, 0, i)),
            pl.BlockSpec((None, tm, KV_LATENT), lambda b, i: (b, i, 0)),
            pl.BlockSpec((None, tm // KEY_TILE, KV_LATENT + ONES_ROWS, KEY_TILE), lambda b, i: (b, i, 0, 0)),
            pl.BlockSpec((None, COL_QI, tm), lambda b, i: (b, 0, i)),
            pl.BlockSpec((None, tm, IDX_HEAD_DIM), lambda b, i: (b, i, 0)),
            pl.BlockSpec((None, N_IDX_HEADS, tm), lambda b, i: (b, 0, i)),
            pl.BlockSpec((None, tm, POOL_WIDTH), lambda b, i: (b, i, 0)),
        ],
        out_shape=[
            jax.ShapeDtypeStruct((B, N_ATT_HEADS, KV_LATENT, S), BF16),
            jax.ShapeDtypeStruct((B, S, KV_LATENT), BF16),
            jax.ShapeDtypeStruct((B, S // KEY_TILE, KV_LATENT + ONES_ROWS, KEY_TILE), BF16),
            jax.ShapeDtypeStruct((B, COL_QI, S), BF16),
            jax.ShapeDtypeStruct((B, S, IDX_HEAD_DIM), BF16),
            jax.ShapeDtypeStruct((B, N_IDX_HEADS, S), F32),
            jax.ShapeDtypeStruct((B, S, POOL_WIDTH), F32),
        ],
        compiler_params=pltpu.CompilerParams(
            dimension_semantics=("parallel", "parallel"), vmem_limit_bytes=VMEM_LIMIT),
        name="proj",
    )(x, g, w_cat, wuk, gkv)


def _any(mask):
    return jnp.max(jnp.where(mask, 1.0, 0.0)) > 0.0


def _fold_rows(x, op, ways=FOLD_WAYS):
    groups = [x[g * SUBLANES:(g + 1) * SUBLANES] for g in range(x.shape[0] // SUBLANES)]
    parts = groups[:ways]
    for g, grp in enumerate(groups[ways:]):
        parts[g % ways] = op(parts[g % ways], grp)
    while len(parts) > 1:
        parts = [op(parts[i], parts[i + 1]) if i + 1 < len(parts) else parts[i] for i in range(0, len(parts), 2)]
    return parts[0]


def _attn_kernel(qlt_ref, qit_ref, wit_ref, kx_ref, c_ref, cxt_ref, wuvt_ref, a_ref,
                 sm_ref, dm_ref, lg_ref, mx_ref, ox_ref, *, n_top):
    qb = qit_ref.shape[1]
    n_tiles, _, tk = cxt_ref.shape
    dc = c_ref.shape[1]
    j = pl.program_id(1)
    nk = (j + 1) * (qb // tk)

    qpos = j * qb + lax.broadcasted_iota(jnp.int32, (1, qb), 1)
    qchunk = qpos // CHUNK
    n_adm = ((qchunk + 1) * CHUNK).astype(F32)
    k_sel = jnp.minimum(n_adm, float(n_top))
    pos_inf = jnp.full((1, qb), jnp.inf, F32)
    group_zeros = jnp.zeros((SUBLANES, qb), F32)
    group_inf = jnp.full((SUBLANES, qb), jnp.inf, F32)

    def kpos_of(t, n=tk):
        return t * n + lax.broadcasted_iota(jnp.int32, (n, 1), 0)

    n_pairs = (nk + 1) // 2

    def half_of(t):
        return pl.ds(pl.multiple_of((t % 2) * tk, tk), tk)

    @pl.when(nk % 2 == 1)
    def _():
        sm_ref[nk // 2, tk:, :] = jnp.full((tk, qb), -jnp.inf, F32)

    def keys_of(ref, t):
        return ref[pl.ds(pl.multiple_of(t * tk, tk), tk), :]

    def col_sum(x):
        return jnp.sum(x, axis=0, keepdims=True)

    def loop2(n, body, init):
        carry = lax.fori_loop(0, n // 2, lambda i, c: body(2 * i + 1, body(2 * i, c)), init)
        return lax.cond(n % 2 == 1, lambda c: body(n - 1, c), lambda c: c, carry)

    def loop4(n, body, init):
        def four(i, c):
            for u in range(4):
                c = body(4 * i + u, c)
            return c
        carry = lax.fori_loop(0, n // 4, four, init)
        base = (n // 4) * 4
        carry = lax.cond(n % 4 >= 2, lambda c: body(base + 1, body(base, c)), lambda c: c, carry)
        return lax.cond(n % 2 == 1, lambda c: body(n - 1, c), lambda c: c, carry)

    wit = wit_ref[...]

    def score_tile(t, carry, own_chunks):
        mn, mx = carry
        kx = keys_of(kx_ref, t)
        score = None
        for i in range(N_IDX_HEADS):
            rel = jnp.dot(kx, qit_ref[i * IDX_HEAD_DIM:(i + 1) * IDX_HEAD_DIM, :],
                          preferred_element_type=F32)
            term = wit[i:i + 1, :] * jnp.maximum(rel, 0.0)
            score = term if score is None else score + term
        if own_chunks:
            adm = (kpos_of(t) // CHUNK) <= qchunk
            lowest, score = jnp.where(adm, score, jnp.inf), jnp.where(adm, score, -jnp.inf)
        else:
            lowest = score
        sm_ref[t // 2, half_of(t), :] = score
        return (jnp.minimum(mn, _fold_rows(lowest, jnp.minimum)), jnp.maximum(mx, _fold_rows(score, jnp.maximum)))

    n_own = qb // tk
    carry = loop4(nk - n_own, functools.partial(score_tile, own_chunks=False), (group_inf, -group_inf))
    for own in range(n_own):
        carry = score_tile(nk - n_own + own, carry, own_chunks=True)
    mn, mx = carry
    q_min = jnp.min(mn, axis=0, keepdims=True)
    q_max = jnp.max(mx, axis=0, keepdims=True)

    state = (q_min, q_max + (jnp.abs(q_max) * (2.0 ** -20) + 1e-30), n_adm, jnp.zeros_like(n_adm))

    def count_ge(mid):
        def body(pr, acc):
            return acc + _fold_rows(jnp.where(sm_ref[pr] >= mid, 1.0, 0.0), jnp.add)
        return col_sum(loop2(n_pairs, body, group_zeros))

    def bisect(n, st):
        lo, hi, clo, chi = st
        for _ in range(n):
            mid = lo + 0.5 * (hi - lo)
            cnt = count_ge(mid)
            ge = cnt >= k_sel
            lo, hi = jnp.where(ge, mid, lo), jnp.where(ge, hi, mid)
            clo, chi = jnp.where(ge, cnt, clo), jnp.where(ge, chi, cnt)
        return lo, hi, clo, chi

    def in_range_ends(st):
        lo, hi = st[0], st[1]

        def body(pr, carry):
            s = sm_ref[pr]
            return (jnp.minimum(carry[0], _fold_rows(jnp.where(s >= lo, s, jnp.inf), jnp.minimum)),
                    jnp.maximum(carry[1], _fold_rows(jnp.where(s < hi, s, -jnp.inf), jnp.maximum)))

        v_lo, v_hi = loop2(n_pairs, body, (group_inf, -group_inf))
        return jnp.min(v_lo, axis=0, keepdims=True), jnp.max(v_hi, axis=0, keepdims=True)

    def peel(st):
        lo, hi, clo, chi = st
        v_lo, v_hi = in_range_ends(st)
        done = clo == k_sel
        one_short = jnp.logical_and(jnp.logical_not(done), k_sel - chi == 1.0)
        one_over = jnp.logical_and(jnp.logical_not(jnp.logical_or(done, one_short)), clo - k_sel == 1.0)
        settled = jnp.logical_or(jnp.logical_or(done, one_short), jnp.logical_or(one_over, v_lo == v_hi))
        return (jnp.where(one_short, v_hi, lo), jnp.where(one_over, v_lo, pos_inf),
                _any(jnp.logical_not(settled)))

    state = lax.cond(nk * tk > n_top, functools.partial(bisect, FIRST_BISECT_STEPS), lambda st: st, state)

    def refine(carry):
        _, rounds, st, _, _ = carry
        lo_sel, drop, is_open = peel(st)
        st = lax.cond(is_open, functools.partial(bisect, MORE_BISECT_STEPS), lambda s: s, st)
        return is_open.astype(jnp.int32), rounds + 1, st, lo_sel, drop

    undecided = _any(state[2] != k_sel)
    _, _, state, lo_sel, drop = lax.while_loop(
        lambda c: jnp.logical_and(c[0] > 0, c[1] < MAX_BISECT_ROUNDS), refine,
        (undecided.astype(jnp.int32), jnp.int32(0), state, state[0], pos_inf))

    def select_pair(pr, acc):
        s = sm_ref[pr]
        sel = jnp.logical_and(s >= lo_sel, s != drop)
        dist = jnp.abs(qpos - kpos_of(pr, 2 * tk)).astype(F32)
        dm_ref[pr] = jnp.where(sel, dist, MASKED_DIST)
        return acc + _fold_rows(jnp.where(sel, 1.0, 0.0), jnp.add)

    n_sel = col_sum(loop2(n_pairs, select_pair, group_zeros))

    wrong = n_sel != k_sel

    @pl.when(_any(wrong))
    def _():
        def unresolved(st):
            v_lo, v_hi = in_range_ends(st)
            open_q = jnp.logical_and(wrong, jnp.logical_and(st[2] != k_sel, v_lo != v_hi))
            return _any(open_q).astype(jnp.int32)

        def tighten(carry):
            _, rounds, st = carry
            st = bisect(TIE_BISECT_STEPS, st)
            return unresolved(st), rounds + 1, st

        _, _, (lo, hi, _, chi) = lax.while_loop(
            lambda c: jnp.logical_and(c[0] > 0, c[1] < MAX_BISECT_ROUNDS), tighten,
            (unresolved(state), jnp.int32(0), state))
        need = k_sel - chi
        key_i = lax.broadcasted_iota(jnp.int32, (tk, tk), 0)
        tri = jnp.where(key_i >= lax.broadcasted_iota(jnp.int32, (tk, tk), 1), 1.0, 0.0).astype(BF16)

        def reselect_tile(t, seen):
            s = sm_ref[t // 2, half_of(t), :]
            in_range = jnp.logical_and(s >= lo, s < hi)
            ones = jnp.where(in_range, 1.0, 0.0)
            rank = seen + jnp.dot(tri, ones.astype(BF16), preferred_element_type=F32)
            sel_x = jnp.logical_or(s >= hi, jnp.logical_and(in_range, rank <= need))
            exact = jnp.where(sel_x, jnp.abs(qpos - kpos_of(t)).astype(F32), MASKED_DIST)
            dm_ref[t // 2, half_of(t), :] = jnp.where(wrong, exact, dm_ref[t // 2, half_of(t), :])
            return seen + col_sum(_fold_rows(ones, jnp.add))

        loop2(nk, reselect_tile, jnp.zeros((1, qb), F32))

    slopes = [LOG2E * 2.0 ** (-8.0 * (hd + 1) / N_ATT_HEADS) for hd in range(N_ATT_HEADS)]
    for hd in range(N_ATT_HEADS):
        mx_ref[hd] = -group_inf
        ox_ref[hd] = jnp.zeros(ox_ref.shape[1:], F32)

    def top_tile(t, _):
        dm = dm_ref[t // 2, half_of(t), :]
        c = keys_of(c_ref, t)
        for hd in range(N_ATT_HEADS):
            logits = jnp.dot(c, qlt_ref[hd], preferred_element_type=F32) - slopes[hd] * dm
            lg_ref[hd, t] = logits
            mx_ref[hd] = jnp.maximum(mx_ref[hd], _fold_rows(logits, jnp.maximum, ways=1))
        return 0

    loop4(nk, top_tile, 0)
    q_top = [jnp.max(mx_ref[hd], axis=0, keepdims=True) for hd in range(N_ATT_HEADS)]

    def pv_tile(t, _):
        cxt = cxt_ref[t]
        for hd in range(N_ATT_HEADS):
            p = jnp.exp2(lg_ref[hd, t] - q_top[hd]).astype(BF16)
            ox_ref[hd] += jnp.dot(cxt, p, preferred_element_type=F32)
        return 0

    loop4(nk, pv_tile, 0)
    a_heads = []
    for hd in range(N_ATT_HEADS):
        ox = ox_ref[hd]
        o_t = (ox[:dc] / ox[dc:dc + 1]).astype(BF16)
        a_heads.append(jnp.dot(wuvt_ref[hd], o_t, preferred_element_type=F32))
    a_ref[...] = jnp.concatenate(a_heads, axis=0).T.astype(BF16)


def _attn(qlt, qit, wit, kx, c, cxt, wuvt, qb, n_top):
    B, H, dc, S = qlt.shape
    n_tiles, ox_rows, tk = cxt.shape[1:]
    grid = (B, S // qb)
    return pl.pallas_call(
        functools.partial(_attn_kernel, n_top=n_top),
        grid=grid,
        in_specs=[
            pl.BlockSpec((None, H, dc, qb), lambda b, j: (b, 0, 0, j)),
            pl.BlockSpec((None, COL_QI, qb), lambda b, j: (b, 0, j)),
            pl.BlockSpec((None, N_IDX_HEADS, qb), lambda b, j: (b, 0, j)),
            pl.BlockSpec((None, S, IDX_HEAD_DIM), lambda b, j: (b, 0, 0)),
            pl.BlockSpec((None, S, dc), lambda b, j: (b, 0, 0)),
            pl.BlockSpec((None, n_tiles, ox_rows, tk), lambda b, j: (b, 0, 0, 0)),
            pl.BlockSpec((H, ATT_HEAD_DIM, dc), lambda b, j: (0, 0, 0)),
        ],
        out_specs=pl.BlockSpec((None, qb, COL_Q), lambda b, j: (b, j, 0)),
        out_shape=jax.ShapeDtypeStruct((B, S, COL_Q), BF16),
        scratch_shapes=[
            pltpu.VMEM(((n_tiles + 1) // 2, 2 * tk, qb), F32),
            pltpu.VMEM(((n_tiles + 1) // 2, 2 * tk, qb), F32),
            pltpu.VMEM((H, n_tiles, tk, qb), F32),
            pltpu.VMEM((H, SUBLANES, qb), F32),
            pltpu.VMEM((H, ox_rows, qb), F32),
        ],
        compiler_params=pltpu.CompilerParams(
            dimension_semantics=("parallel", "parallel"), vmem_limit_bytes=VMEM_LIMIT),
        name="attn",
    )(qlt, qit, wit, kx, c, cxt, wuvt)


def _mixffn_kernel(x_ref, xh_ref, a_ref, ah_ref, u_ref, uh_ref, wp_ref, ps_ref, wo_ref,
                   g_ref, wup_ref, cw_ref, cb_ref, wdn_ref, gf_ref, o_ref, act_ref, *, final_norm):
    tm, D = x_ref.shape
    n_chunks, _, fc2 = wup_ref.shape
    fc = fc2 // 2
    i = pl.program_id(1)
    rows = CONV_HALO + tm
    first = i == 0

    u_hist = jnp.where(first, 0.0, uh_ref[...])
    ext = jnp.concatenate([u_hist, u_ref[...]], axis=0)
    t = i * tm - CONV_HALO + lax.broadcasted_iota(jnp.int32, (rows, 1), 0)
    mixed = []
    for g, win in enumerate(POOL_WINDOWS):
        e = ext[:, g * POOL_GROUP:(g + 1) * POOL_GROUP]
        acc = e
        span = 1
        while span < win:
            acc = acc + pltpu.roll(acc, span, axis=0)
            span *= 2
        count = jnp.clip(t + 1, 1, win).astype(F32)
        pooled = acc[U_HALO - CONV_HALO:] / count - e[U_HALO - CONV_HALO:]
        mixed.append(jnp.dot(pooled.astype(BF16), wp_ref[g], preferred_element_type=F32))
    b = (jnp.concatenate(mixed, axis=1) * ps_ref[...]).astype(BF16)
    a = jnp.concatenate([ah_ref[A_HALO - CONV_HALO:], a_ref[...]], axis=0)
    xe = jnp.concatenate([xh_ref[...], x_ref[...]], axis=0)
    xe = xe + jnp.dot(jnp.concatenate([a, b], axis=1), wo_ref[...], preferred_element_type=F32)
    x = xe[CONV_HALO:]

    h = _rms(xe, g_ref[...])
    row = lax.broadcasted_iota(jnp.int32, (rows, 1), 0)
    h = jnp.where(jnp.logical_and(first, row < CONV_HALO), 0.0, h).astype(BF16)

    for k in range(n_chunks):
        up = jnp.dot(h, wup_ref[k], preferred_element_type=F32)
        cw = cw_ref[k]
        conv = cb_ref[k] + cw[2:3] * up
        for jj in range(CONV_WIDTH - 1):
            conv = conv + cw[jj:jj + 1] * pltpu.roll(up, CONV_WIDTH - 1 - jj, axis=0)
        conv = conv[CONV_HALO:]
        gate = conv[:, :fc]
        act = gate * (1.0 / (1.0 + jnp.exp(-gate))) * conv[:, fc:]
        act_ref[:, k * fc:(k + 1) * fc] = act.astype(BF16)

    y = x + jnp.dot(act_ref[...], wdn_ref[...], preferred_element_type=F32)
    if final_norm:
        y = _rms(y, gf_ref[...])
    o_ref[...] = y


def _mixffn(x, a, u, w_pool, pool_scale, w_o, g, wup, cw, cb, wdn, g_final, tm, final_norm):
    B, S, D = x.shape
    grid = (B, S // tm)
    hist = lambda rows: (lambda b, i: (b, jnp.maximum(i * (tm // rows) - 1, 0), 0))
    tile = lambda b, i: (b, i, 0)
    const = lambda *shape: pl.BlockSpec(shape, lambda b, i: (0,) * len(shape), pipeline_mode=pl.Buffered(1))
    return pl.pallas_call(
        functools.partial(_mixffn_kernel, final_norm=final_norm),
        grid=grid,
        in_specs=[
            pl.BlockSpec((None, tm, D), tile),
            pl.BlockSpec((None, CONV_HALO, D), hist(CONV_HALO)),
            pl.BlockSpec((None, tm, COL_Q), tile),
            pl.BlockSpec((None, A_HALO, COL_Q), hist(A_HALO)),
            pl.BlockSpec((None, tm, POOL_WIDTH), tile),
            pl.BlockSpec((None, U_HALO, POOL_WIDTH), hist(U_HALO)),
            const(*w_pool.shape),
            const(1, POOL_WIDTH),
            const(*w_o.shape),
            const(1, D),
            const(*wup.shape),
            const(*cw.shape),
            const(*cb.shape),
            const(*wdn.shape),
            const(1, D),
        ],
        out_specs=pl.BlockSpec((None, tm, D), tile),
        out_shape=jax.ShapeDtypeStruct((B, S, D), F32),
        scratch_shapes=[pltpu.VMEM((tm, wdn.shape[0]), BF16)],
        compiler_params=pltpu.CompilerParams(
            dimension_semantics=("parallel", "parallel"), vmem_limit_bytes=VMEM_LIMIT),
        name="mixffn",
    )(x, x, a, a, u, u, w_pool, pool_scale, w_o, g, wup, cw, cb, wdn, g_final)


def _ffn_chunk(d_ff):
    for fc in (256, 128):
        if d_ff % fc == 0:
            return fc
    raise ValueError(f"unsupported FFN width {d_ff}")


def kernel(x, g_mix, w_in, g_kv, w_uk, w_uv, w_pool, pool_scale, w_o, g_ffn, w_up, conv_w, conv_b, w_down, g_final):
    B, S, D = x.shape
    depth = g_mix.shape[0]
    d_ff = w_down.shape[1]
    fc = _ffn_chunk(d_ff)
    n_chunks = d_ff // fc
    n_top = min(TOPK_MAX, S // 4)
    tm = min(512, S)
    qb = min(256, S)
    assert S % tm == 0 and S % qb == 0 and qb % KEY_TILE == 0 and tm % KEY_TILE == 0 and KEY_TILE % CHUNK == 0
    assert w_in.shape[2] == COL_Q + KV_LATENT + COL_QI + IDX_HEAD_DIM + N_IDX_HEADS + POOL_WIDTH

    s_q = COL_Q
    s_c = s_q + KV_LATENT
    s_qi = s_c + COL_QI
    s_ki = s_qi + IDX_HEAD_DIM
    s_wi = s_ki + N_IDX_HEADS
    row = lambda v: v.reshape(1, -1).astype(F32)

    for l in range(depth):
        w = w_in[l]
        w_cat = jnp.concatenate(
            [w[:, :s_q], w[:, s_q:s_c], w[:, s_c:s_qi], w[:, s_wi:], w[:, s_qi:s_ki], w[:, s_ki:s_wi],
             jnp.zeros((D, TAIL - IDX_HEAD_DIM - N_IDX_HEADS), w.dtype)], axis=1).astype(BF16)
        wuvt = jnp.swapaxes(w_uv[l], 1, 2).astype(BF16)
        pair = lambda m: jnp.concatenate(
            [m[..., :d_ff].reshape(m.shape[:-1] + (n_chunks, fc)),
             m[..., d_ff:].reshape(m.shape[:-1] + (n_chunks, fc))], axis=-1)
        wup = jnp.moveaxis(pair(w_up[l]), 1, 0).astype(BF16)
        cw = jnp.moveaxis(pair(conv_w[l]), 1, 0).astype(F32)
        cb = pair(conv_b[l]).reshape(n_chunks, 1, 2 * fc).astype(F32)
        wdn = w_down[l].astype(BF16)

        qlt, c, cxt, qit, kx, wit, u = _proj(x, row(g_mix[l]), w_cat, w_uk[l].astype(BF16), row(g_kv[l]), tm)
        a = _attn(qlt, qit, wit, kx, c, cxt, wuvt, qb, n_top)
        x = _mixffn(x, a, u, w_pool[l].astype(BF16), row(pool_scale[l]), w_o[l].astype(BF16),
                    row(g_ffn[l]), wup, cw, cb, wdn, row(g_final), tm, final_norm=(l == depth - 1))
    return x
```

```python
import functools

import jax
import jax.numpy as jnp
from jax import lax
from jax.experimental import pallas as pl
from jax.experimental.pallas import tpu as pltpu

CHUNK = 64
N_ATT_HEADS = 8
ATT_HEAD_DIM = 64
KV_LATENT = 128
N_IDX_HEADS = 8
IDX_HEAD_DIM = 64
TOPK_MAX = 256
POOL_WINDOWS = (2, 4, 8, 16)
POOL_GROUP = 128
CONV_WIDTH = 3
EPS = 1e-6

COL_Q = N_ATT_HEADS * ATT_HEAD_DIM
COL_QI = N_IDX_HEADS * IDX_HEAD_DIM
POOL_WIDTH = POOL_GROUP * len(POOL_WINDOWS)
TAIL = 128
U_HALO = 32
A_HALO = 16
CONV_HALO = 8
MASKED_DIST = 1e30
FIRST_BISECT_STEPS = 14
MORE_BISECT_STEPS = 2
TIE_BISECT_STEPS = 8
MAX_BISECT_ROUNDS = 100
ONES_ROWS = 16
SUBLANES = 8
FOLD_WAYS = 4
KEY_TILE = 256
LOG2E = 1.4426950408889634
VMEM_LIMIT = 56 * 1024 * 1024

F32 = jnp.float32
BF16 = jnp.bfloat16


def _rms(x, g):
    return x * lax.rsqrt(jnp.mean(x * x, axis=-1, keepdims=True) + EPS) * g


def _proj_kernel(x_ref, g_ref, w_ref, wuk_ref, gkv_ref,
                 qlt_ref, c_ref, cxt_ref, qit_ref, kx_ref, wit_ref, u_ref):
    h = _rms(x_ref[...], g_ref[...]).astype(BF16)
    p = jnp.dot(h, w_ref[...], preferred_element_type=F32)
    tm = p.shape[0]
    o_c = COL_Q
    o_qi = o_c + KV_LATENT
    o_u = o_qi + COL_QI
    o_t = o_u + POOL_WIDTH
    q = p[:, :o_c].astype(BF16)
    for hd in range(N_ATT_HEADS):
        ql = jnp.dot(q[:, hd * ATT_HEAD_DIM:(hd + 1) * ATT_HEAD_DIM], wuk_ref[hd],
                     preferred_element_type=F32)
        qlt_ref[hd] = (ql * (ATT_HEAD_DIM ** -0.5 * LOG2E)).T.astype(BF16)
    c = _rms(p[:, o_c:o_qi], gkv_ref[...])
    c_ref[...] = c.astype(BF16)
    ones_rows = jnp.where(lax.broadcasted_iota(jnp.int32, (ONES_ROWS, tm), 0) == 0, 1.0, 0.0)
    cxt = jnp.concatenate([c.T, ones_rows], axis=0).astype(BF16)
    for tt in range(cxt_ref.shape[0]):
        cxt_ref[tt] = cxt[:, tt * KEY_TILE:(tt + 1) * KEY_TILE]
    qit_ref[...] = p[:, o_qi:o_u].T.astype(BF16)
    u_ref[...] = p[:, o_u:o_t]
    tail = p[:, o_t:o_t + TAIL]
    kx_ref[...] = tail[:, :IDX_HEAD_DIM].astype(BF16)
    wit_ref[...] = tail.T[IDX_HEAD_DIM:IDX_HEAD_DIM + N_IDX_HEADS] * (
        (N_IDX_HEADS ** -0.5) * (IDX_HEAD_DIM ** -0.5))


def _proj(x, g, w_cat, wuk, gkv, tm):
    B, S, D = x.shape
    ncol = w_cat.shape[1]
    grid = (B, S // tm)
    const = lambda *shape: pl.BlockSpec(shape, lambda b, i: (0,) * len(shape))
    return pl.pallas_call(
        _proj_kernel,
        grid=grid,
        in_specs=[
            pl.BlockSpec((None, tm, D), lambda b, i: (b, i, 0)),
            const(1, D),
            const(D, ncol),
            const(N_ATT_HEADS, ATT_HEAD_DIM, KV_LATENT),
            const(1, KV_LATENT),
        ],
        out_specs=[
            pl.BlockSpec((None, N_ATT_HEADS, KV_LATENT, tm), lambda b, i: (b, 0, 0, i)),
            pl.BlockSpec((None, tm, KV_LATENT), lambda b, i: (b, i, 0)),
            pl.BlockSpec((None, tm // KEY_TILE, KV_LATENT + ONES_ROWS, KEY_TILE), lambda b, i: (b, i, 0, 0)),
            pl.BlockSpec((None, COL_QI, tm), lambda b, i: (b, 0, i)),
            pl.BlockSpec((None, tm, IDX_HEAD_DIM), lambda b, i: (b, i, 0)),
            pl.BlockSpec((None, N_IDX_HEADS, tm), lambda b, i: (b, 0, i)),
            pl.BlockSpec((None, tm, POOL_WIDTH), lambda b, i: (b, i, 0)),
        ],
        out_shape=[
            jax.ShapeDtypeStruct((B, N_ATT_HEADS, KV_LATENT, S), BF16),
            jax.ShapeDtypeStruct((B, S, KV_LATENT), BF16),
            jax.ShapeDtypeStruct((B, S // KEY_TILE, KV_LATENT + ONES_ROWS, KEY_TILE), BF16),
            jax.ShapeDtypeStruct((B, COL_QI, S), BF16),
            jax.ShapeDtypeStruct((B, S, IDX_HEAD_DIM), BF16),
            jax.ShapeDtypeStruct((B, N_IDX_HEADS, S), F32),
            jax.ShapeDtypeStruct((B, S, POOL_WIDTH), F32),
        ],
        compiler_params=pltpu.CompilerParams(
            dimension_semantics=("parallel", "parallel"), vmem_limit_bytes=VMEM_LIMIT),
        name="proj",
    )(x, g, w_cat, wuk, gkv)


def _any(mask):
    return jnp.max(jnp.where(mask, 1.0, 0.0)) > 0.0


def _fold_rows(x, op, ways=FOLD_WAYS):
    groups = [x[g * SUBLANES:(g + 1) * SUBLANES] for g in range(x.shape[0] // SUBLANES)]
    parts = groups[:ways]
    for g, grp in enumerate(groups[ways:]):
        parts[g % ways] = op(parts[g % ways], grp)
    while len(parts) > 1:
        parts = [op(parts[i], parts[i + 1]) if i + 1 < len(parts) else parts[i] for i in range(0, len(parts), 2)]
    return parts[0]


def _attn_kernel(qlt_ref, qit_ref, wit_ref, kx_ref, c_ref, cxt_ref, wuvt_ref, a_ref,
                 sm_ref, dm_ref, lg_ref, mx_ref, ox_ref, *, n_top):
    qb = qit_ref.shape[1]
    n_tiles, _, tk = cxt_ref.shape
    dc = c_ref.shape[1]
    j = pl.program_id(1)
    nk = (j + 1) * (qb // tk)

    qpos = j * qb + lax.broadcasted_iota(jnp.int32, (1, qb), 1)
    qchunk = qpos // CHUNK
    n_adm = ((qchunk + 1) * CHUNK).astype(F32)
    k_sel = jnp.minimum(n_adm, float(n_top))
    pos_inf = jnp.full((1, qb), jnp.inf, F32)
    group_zeros = jnp.zeros((SUBLANES, qb), F32)
    group_inf = jnp.full((SUBLANES, qb), jnp.inf, F32)

    def kpos_of(t, n=tk):
        return t * n + lax.broadcasted_iota(jnp.int32, (n, 1), 0)

    n_pairs = (nk + 1) // 2

    def half_of(t):
        return pl.ds(pl.multiple_of((t % 2) * tk, tk), tk)

    @pl.when(nk % 2 == 1)
    def _():
        sm_ref[nk // 2, tk:, :] = jnp.full((tk, qb), -jnp.inf, F32)

    def keys_of(ref, t):
        return ref[pl.ds(pl.multiple_of(t * tk, tk), tk), :]

    def col_sum(x):
        return jnp.sum(x, axis=0, keepdims=True)

    def loop2(n, body, init):
        carry = lax.fori_loop(0, n // 2, lambda i, c: body(2 * i + 1, body(2 * i, c)), init)
        return lax.cond(n % 2 == 1, lambda c: body(n - 1, c), lambda c: c, carry)

    def loop4(n, body, init):
        def four(i, c):
            for u in range(4):
                c = body(4 * i + u, c)
            return c
        carry = lax.fori_loop(0, n // 4, four, init)
        base = (n // 4) * 4
        carry = lax.cond(n % 4 >= 2, lambda c: body(base + 1, body(base, c)), lambda c: c, carry)
        return lax.cond(n % 2 == 1, lambda c: body(n - 1, c), lambda c: c, carry)

    wit = wit_ref[...]

    def score_tile(t, carry, own_chunks):
        mn, mx = carry
        kx = keys_of(kx_ref, t)
        score = None
        for i in range(N_IDX_HEADS):
            rel = jnp.dot(kx, qit_ref[i * IDX_HEAD_DIM:(i + 1) * IDX_HEAD_DIM, :],
                          preferred_element_type=F32)
            term = wit[i:i + 1, :] * jnp.maximum(rel, 0.0)
            score = term if score is None else score + term
        if own_chunks:
            adm = (kpos_of(t) // CHUNK) <= qchunk
            lowest, score = jnp.where(adm, score, jnp.inf), jnp.where(adm, score, -jnp.inf)
        else:
            lowest = score
        sm_ref[t // 2, half_of(t), :] = score
        return (jnp.minimum(mn, _fold_rows(lowest, jnp.minimum)), jnp.maximum(mx, _fold_rows(score, jnp.maximum)))

    n_own = qb // tk
    carry = loop4(nk - n_own, functools.partial(score_tile, own_chunks=False), (group_inf, -group_inf))
    for own in range(n_own):
        carry = score_tile(nk - n_own + own, carry, own_chunks=True)
    mn, mx = carry
    q_min = jnp.min(mn, axis=0, keepdims=True)
    q_max = jnp.max(mx, axis=0, keepdims=True)

    state = (q_min, q_max + (jnp.abs(q_max) * (2.0 ** -20) + 1e-30), n_adm, jnp.zeros_like(n_adm))

    def count_ge(mid, pairs=None):
        def body(pr, acc):
            return acc + _fold_rows(jnp.where(sm_ref[pr] >= mid, 1.0, 0.0), jnp.add)
        if pairs is None:
            return col_sum(loop2(n_pairs, body, group_zeros))
        acc = group_zeros
        for pr in range(pairs):
            acc = body(pr, acc)
        return col_sum(acc)

    def bisect(n, st, pairs=None):
        lo, hi, clo, chi = st
        for _ in range(n):
            mid = lo + 0.5 * (hi - lo)
            cnt = count_ge(mid, pairs)
            ge = cnt >= k_sel
            lo, hi = jnp.where(ge, mid, lo), jnp.where(ge, hi, mid)
            clo, chi = jnp.where(ge, cnt, clo), jnp.where(ge, chi, cnt)
        return lo, hi, clo, chi

    def in_range_ends(st):
        lo, hi = st[0], st[1]

        def body(pr, carry):
            s = sm_ref[pr]
            return (jnp.minimum(carry[0], _fold_rows(jnp.where(s >= lo, s, jnp.inf), jnp.minimum)),
                    jnp.maximum(carry[1], _fold_rows(jnp.where(s < hi, s, -jnp.inf), jnp.maximum)))

        v_lo, v_hi = loop2(n_pairs, body, (group_inf, -group_inf))
        return jnp.min(v_lo, axis=0, keepdims=True), jnp.max(v_hi, axis=0, keepdims=True)

    def peel(st):
        lo, hi, clo, chi = st
        v_lo, v_hi = in_range_ends(st)
        done = clo == k_sel
        one_short = jnp.logical_and(jnp.logical_not(done), k_sel - chi == 1.0)
        one_over = jnp.logical_and(jnp.logical_not(jnp.logical_or(done, one_short)), clo - k_sel == 1.0)
        settled = jnp.logical_or(jnp.logical_or(done, one_short), jnp.logical_or(one_over, v_lo == v_hi))
        return (jnp.where(one_short, v_hi, lo), jnp.where(one_over, v_lo, pos_inf),
                _any(jnp.logical_not(settled)))

    def first_steps(st, pairs=1):
        run = functools.partial(bisect, FIRST_BISECT_STEPS, pairs=pairs)
        if pairs == (n_tiles + 1) // 2:
            return run(st)
        return lax.cond(n_pairs == pairs, run, functools.partial(first_steps, pairs=pairs + 1), st)

    state = lax.cond(nk * tk > n_top, first_steps, lambda st: st, state)

    def refine(carry):
        _, rounds, st, _, _ = carry
        lo_sel, drop, is_open = peel(st)
        st = lax.cond(is_open, functools.partial(bisect, MORE_BISECT_STEPS), lambda s: s, st)
        return is_open.astype(jnp.int32), rounds + 1, st, lo_sel, drop

    undecided = _any(state[2] != k_sel)
    _, _, state, lo_sel, drop = lax.while_loop(
        lambda c: jnp.logical_and(c[0] > 0, c[1] < MAX_BISECT_ROUNDS), refine,
        (undecided.astype(jnp.int32), jnp.int32(0), state, state[0], pos_inf))

    def select_pair(pr, acc):
        s = sm_ref[pr]
        sel = jnp.logical_and(s >= lo_sel, s != drop)
        dist = jnp.abs(qpos - kpos_of(pr, 2 * tk)).astype(F32)
        dm_ref[pr] = jnp.where(sel, dist, MASKED_DIST)
        return acc + _fold_rows(jnp.where(sel, 1.0, 0.0), jnp.add)

    n_sel = col_sum(loop2(n_pairs, select_pair, group_zeros))

    wrong = n_sel != k_sel

    @pl.when(_any(wrong))
    def _():
        def unresolved(st):
            v_lo, v_hi = in_range_ends(st)
            open_q = jnp.logical_and(wrong, jnp.logical_and(st[2] != k_sel, v_lo != v_hi))
            return _any(open_q).astype(jnp.int32)

        def tighten(carry):
            _, rounds, st = carry
            st = bisect(TIE_BISECT_STEPS, st)
            return unresolved(st), rounds + 1, st

        _, _, (lo, hi, _, chi) = lax.while_loop(
            lambda c: jnp.logical_and(c[0] > 0, c[1] < MAX_BISECT_ROUNDS), tighten,
            (unresolved(state), jnp.int32(0), state))
        need = k_sel - chi
        key_i = lax.broadcasted_iota(jnp.int32, (tk, tk), 0)
        tri = jnp.where(key_i >= lax.broadcasted_iota(jnp.int32, (tk, tk), 1), 1.0, 0.0).astype(BF16)

        def reselect_tile(t, seen):
            s = sm_ref[t // 2, half_of(t), :]
            in_range = jnp.logical_and(s >= lo, s < hi)
            ones = jnp.where(in_range, 1.0, 0.0)
            rank = seen + jnp.dot(tri, ones.astype(BF16), preferred_element_type=F32)
            sel_x = jnp.logical_or(s >= hi, jnp.logical_and(in_range, rank <= need))
            exact = jnp.where(sel_x, jnp.abs(qpos - kpos_of(t)).astype(F32), MASKED_DIST)
            dm_ref[t // 2, half_of(t), :] = jnp.where(wrong, exact, dm_ref[t // 2, half_of(t), :])
            return seen + col_sum(_fold_rows(ones, jnp.add))

        loop2(nk, reselect_tile, jnp.zeros((1, qb), F32))

    slopes = [LOG2E * 2.0 ** (-8.0 * (hd + 1) / N_ATT_HEADS) for hd in range(N_ATT_HEADS)]
    for hd in range(N_ATT_HEADS):
        mx_ref[hd] = -group_inf
        ox_ref[hd] = jnp.zeros(ox_ref.shape[1:], F32)

    def top_tile(t, _):
        dm = dm_ref[t // 2, half_of(t), :]
        c = keys_of(c_ref, t)
        for hd in range(N_ATT_HEADS):
            logits = jnp.dot(c, qlt_ref[hd], preferred_element_type=F32) - slopes[hd] * dm
            lg_ref[hd, t] = logits
            mx_ref[hd] = jnp.maximum(mx_ref[hd], _fold_rows(logits, jnp.maximum, ways=1))
        return 0

    loop4(nk, top_tile, 0)
    q_top = [jnp.max(mx_ref[hd], axis=0, keepdims=True) for hd in range(N_ATT_HEADS)]

    def pv_tile(t, _):
        cxt = cxt_ref[t]
        for hd in range(N_ATT_HEADS):
            p = jnp.exp2(lg_ref[hd, t] - q_top[hd]).astype(BF16)
            ox_ref[hd] += jnp.dot(cxt, p, preferred_element_type=F32)
        return 0

    loop4(nk, pv_tile, 0)
    a_heads = []
    for hd in range(N_ATT_HEADS):
        ox = ox_ref[hd]
        o_t = (ox[:dc] / ox[dc:dc + 1]).astype(BF16)
        a_heads.append(jnp.dot(wuvt_ref[hd], o_t, preferred_element_type=F32))
    a_ref[...] = jnp.concatenate(a_heads, axis=0).T.astype(BF16)


def _attn(qlt, qit, wit, kx, c, cxt, wuvt, qb, n_top):
    B, H, dc, S = qlt.shape
    n_tiles, ox_rows, tk = cxt.shape[1:]
    grid = (B, S // qb)
    return pl.pallas_call(
        functools.partial(_attn_kernel, n_top=n_top),
        grid=grid,
        in_specs=[
            pl.BlockSpec((None, H, dc, qb), lambda b, j: (b, 0, 0, j)),
            pl.BlockSpec((None, COL_QI, qb), lambda b, j: (b, 0, j)),
            pl.BlockSpec((None, N_IDX_HEADS, qb), lambda b, j: (b, 0, j)),
            pl.BlockSpec((None, S, IDX_HEAD_DIM), lambda b, j: (b, 0, 0)),
            pl.BlockSpec((None, S, dc), lambda b, j: (b, 0, 0)),
            pl.BlockSpec((None, n_tiles, ox_rows, tk), lambda b, j: (b, 0, 0, 0)),
            pl.BlockSpec((H, ATT_HEAD_DIM, dc), lambda b, j: (0, 0, 0)),
        ],
        out_specs=pl.BlockSpec((None, qb, COL_Q), lambda b, j: (b, j, 0)),
        out_shape=jax.ShapeDtypeStruct((B, S, COL_Q), BF16),
        scratch_shapes=[
            pltpu.VMEM(((n_tiles + 1) // 2, 2 * tk, qb), F32),
            pltpu.VMEM(((n_tiles + 1) // 2, 2 * tk, qb), F32),
            pltpu.VMEM((H, n_tiles, tk, qb), F32),
            pltpu.VMEM((H, SUBLANES, qb), F32),
            pltpu.VMEM((H, ox_rows, qb), F32),
        ],
        compiler_params=pltpu.CompilerParams(
            dimension_semantics=("parallel", "parallel"), vmem_limit_bytes=VMEM_LIMIT),
        name="attn",
    )(qlt, qit, wit, kx, c, cxt, wuvt)


def _mixffn_kernel(x_ref, xh_ref, a_ref, ah_ref, u_ref, uh_ref, wp_ref, ps_ref, wo_ref,
                   g_ref, wup_ref, cw_ref, cb_ref, wdn_ref, gf_ref, o_ref, act_ref, *, final_norm):
    tm, D = x_ref.shape
    n_chunks, _, fc2 = wup_ref.shape
    fc = fc2 // 2
    i = pl.program_id(1)
    rows = CONV_HALO + tm
    first = i == 0

    u_hist = jnp.where(first, 0.0, uh_ref[...])
    ext = jnp.concatenate([u_hist, u_ref[...]], axis=0)
    t = i * tm - CONV_HALO + lax.broadcasted_iota(jnp.int32, (rows, 1), 0)
    mixed = []
    for g, win in enumerate(POOL_WINDOWS):
        e = ext[:, g * POOL_GROUP:(g + 1) * POOL_GROUP]
        acc = e
        span = 1
        while span < win:
            acc = acc + pltpu.roll(acc, span, axis=0)
            span *= 2
        count = jnp.clip(t + 1, 1, win).astype(F32)
        pooled = acc[U_HALO - CONV_HALO:] / count - e[U_HALO - CONV_HALO:]
        mixed.append(jnp.dot(pooled.astype(BF16), wp_ref[g], preferred_element_type=F32))
    b = (jnp.concatenate(mixed, axis=1) * ps_ref[...]).astype(BF16)
    a = jnp.concatenate([ah_ref[A_HALO - CONV_HALO:], a_ref[...]], axis=0)
    xe = jnp.concatenate([xh_ref[...], x_ref[...]], axis=0)
    xe = xe + jnp.dot(jnp.concatenate([a, b], axis=1), wo_ref[...], preferred_element_type=F32)
    x = xe[CONV_HALO:]

    h = _rms(xe, g_ref[...])
    row = lax.broadcasted_iota(jnp.int32, (rows, 1), 0)
    h = jnp.where(jnp.logical_and(first, row < CONV_HALO), 0.0, h).astype(BF16)

    for k in range(n_chunks):
        up = jnp.dot(h, wup_ref[k], preferred_element_type=F32)
        cw = cw_ref[k]
        conv = cb_ref[k] + cw[2:3] * up
        for jj in range(CONV_WIDTH - 1):
            conv = conv + cw[jj:jj + 1] * pltpu.roll(up, CONV_WIDTH - 1 - jj, axis=0)
        conv = conv[CONV_HALO:]
        gate = conv[:, :fc]
        act = gate * (1.0 / (1.0 + jnp.exp(-gate))) * conv[:, fc:]
        act_ref[:, k * fc:(k + 1) * fc] = act.astype(BF16)

    y = x + jnp.dot(act_ref[...], wdn_ref[...], preferred_element_type=F32)
    if final_norm:
        y = _rms(y, gf_ref[...])
    o_ref[...] = y


def _mixffn(x, a, u, w_pool, pool_scale, w_o, g, wup, cw, cb, wdn, g_final, tm, final_norm):
    B, S, D = x.shape
    grid = (B, S // tm)
    hist = lambda rows: (lambda b, i: (b, jnp.maximum(i * (tm // rows) - 1, 0), 0))
    tile = lambda b, i: (b, i, 0)
    const = lambda *shape: pl.BlockSpec(shape, lambda b, i: (0,) * len(shape), pipeline_mode=pl.Buffered(1))
    return pl.pallas_call(
        functools.partial(_mixffn_kernel, final_norm=final_norm),
        grid=grid,
        in_specs=[
            pl.BlockSpec((None, tm, D), tile),
            pl.BlockSpec((None, CONV_HALO, D), hist(CONV_HALO)),
            pl.BlockSpec((None, tm, COL_Q), tile),
            pl.BlockSpec((None, A_HALO, COL_Q), hist(A_HALO)),
            pl.BlockSpec((None, tm, POOL_WIDTH), tile),
            pl.BlockSpec((None, U_HALO, POOL_WIDTH), hist(U_HALO)),
            const(*w_pool.shape),
            const(1, POOL_WIDTH),
            const(*w_o.shape),
            const(1, D),
            const(*wup.shape),
            const(*cw.shape),
            const(*cb.shape),
            const(*wdn.shape),
            const(1, D),
        ],
        out_specs=pl.BlockSpec((None, tm, D), tile),
        out_shape=jax.ShapeDtypeStruct((B, S, D), F32),
        scratch_shapes=[pltpu.VMEM((tm, wdn.shape[0]), BF16)],
        compiler_params=pltpu.CompilerParams(
            dimension_semantics=("parallel", "parallel"), vmem_limit_bytes=VMEM_LIMIT),
        name="mixffn",
    )(x, x, a, a, u, u, w_pool, pool_scale, w_o, g, wup, cw, cb, wdn, g_final)


def _ffn_chunk(d_ff):
    for fc in (256, 128):
        if d_ff % fc == 0:
            return fc
    raise ValueError(f"unsupported FFN width {d_ff}")


def kernel(x, g_mix, w_in, g_kv, w_uk, w_uv, w_pool, pool_scale, w_o, g_ffn, w_up, conv_w, conv_b, w_down, g_final):
    B, S, D = x.shape
    depth = g_mix.shape[0]
    d_ff = w_down.shape[1]
    fc = _ffn_chunk(d_ff)
    n_chunks = d_ff // fc
    n_top = min(TOPK_MAX, S // 4)
    tm = min(512, S)
    qb = min(256, S)
    assert S % tm == 0 and S % qb == 0 and qb % KEY_TILE == 0 and tm % KEY_TILE == 0 and KEY_TILE % CHUNK == 0
    assert w_in.shape[2] == COL_Q + KV_LATENT + COL_QI + IDX_HEAD_DIM + N_IDX_HEADS + POOL_WIDTH

    s_q = COL_Q
    s_c = s_q + KV_LATENT
    s_qi = s_c + COL_QI
    s_ki = s_qi + IDX_HEAD_DIM
    s_wi = s_ki + N_IDX_HEADS
    row = lambda v: v.reshape(1, -1).astype(F32)

    for l in range(depth):
        w = w_in[l]
        w_cat = jnp.concatenate(
            [w[:, :s_q], w[:, s_q:s_c], w[:, s_c:s_qi], w[:, s_wi:], w[:, s_qi:s_ki], w[:, s_ki:s_wi],
             jnp.zeros((D, TAIL - IDX_HEAD_DIM - N_IDX_HEADS), w.dtype)], axis=1).astype(BF16)
        wuvt = jnp.swapaxes(w_uv[l], 1, 2).astype(BF16)
        pair = lambda m: jnp.concatenate(
            [m[..., :d_ff].reshape(m.shape[:-1] + (n_chunks, fc)),
             m[..., d_ff:].reshape(m.shape[:-1] + (n_chunks, fc))], axis=-1)
        wup = jnp.moveaxis(pair(w_up[l]), 1, 0).astype(BF16)
        cw = jnp.moveaxis(pair(conv_w[l]), 1, 0).astype(F32)
        cb = pair(conv_b[l]).reshape(n_chunks, 1, 2 * fc).astype(F32)
        wdn = w_down[l].astype(BF16)

        qlt, c, cxt, qit, kx, wit, u = _proj(x, row(g_mix[l]), w_cat, w_uk[l].astype(BF16), row(g_kv[l]), tm)
        a = _attn(qlt, qit, wit, kx, c, cxt, wuvt, qb, n_top)
        x = _mixffn(x, a, u, w_pool[l].astype(BF16), row(pool_scale[l]), w_o[l].astype(BF16),
                    row(g_ffn[l]), wup, cw, cb, wdn, row(g_final), tm, final_norm=(l == depth - 1))
    return x
```

```python
import functools

import jax
import jax.numpy as jnp
from jax import lax
from jax.experimental import pallas as pl
from jax.experimental.pallas import tpu as pltpu

CHUNK = 64
N_ATT_HEADS = 8
ATT_HEAD_DIM = 64
KV_LATENT = 128
N_IDX_HEADS = 8
IDX_HEAD_DIM = 64
TOPK_MAX = 256
POOL_WINDOWS = (2, 4, 8, 16)
POOL_GROUP = 128
CONV_WIDTH = 3
EPS = 1e-6

COL_Q = N_ATT_HEADS * ATT_HEAD_DIM
COL_QI = N_IDX_HEADS * IDX_HEAD_DIM
POOL_WIDTH = POOL_GROUP * len(POOL_WINDOWS)
TAIL = 128
U_HALO = 32
A_HALO = 16
CONV_HALO = 8
MASKED_DIST = 1e30
FIRST_BISECT_STEPS = 14
MORE_BISECT_STEPS = 2
TIE_BISECT_STEPS = 8
MAX_BISECT_ROUNDS = 100
ONES_ROWS = 16
SUBLANES = 8
FOLD_WAYS = 4
KEY_TILE = 256
LOG2E = 1.4426950408889634
VMEM_LIMIT = 56 * 1024 * 1024

F32 = jnp.float32
BF16 = jnp.bfloat16


def _rms(x, g):
    return x * lax.rsqrt(jnp.mean(x * x, axis=-1, keepdims=True) + EPS) * g


def _proj_kernel(x_ref, g_ref, w_ref, wuk_ref, gkv_ref,
                 qlt_ref, c_ref, cxt_ref, qit_ref, kx_ref, wit_ref, u_ref):
    h = _rms(x_ref[...], g_ref[...]).astype(BF16)
    p = jnp.dot(h, w_ref[...], preferred_element_type=F32)
    tm = p.shape[0]
    o_c = COL_Q
    o_qi = o_c + KV_LATENT
    o_u = o_qi + COL_QI
    o_t = o_u + POOL_WIDTH
    q = p[:, :o_c].astype(BF16)
    for hd in range(N_ATT_HEADS):
        ql = jnp.dot(q[:, hd * ATT_HEAD_DIM:(hd + 1) * ATT_HEAD_DIM], wuk_ref[hd],
                     preferred_element_type=F32)
        qlt_ref[hd] = (ql * (ATT_HEAD_DIM ** -0.5 * LOG2E)).T.astype(BF16)
    c = _rms(p[:, o_c:o_qi], gkv_ref[...])
    c_ref[...] = c.astype(BF16)
    ones_rows = jnp.where(lax.broadcasted_iota(jnp.int32, (ONES_ROWS, tm), 0) == 0, 1.0, 0.0)
    cxt = jnp.concatenate([c.T, ones_rows], axis=0).astype(BF16)
    for tt in range(cxt_ref.shape[0]):
        cxt_ref[tt] = cxt[:, tt * KEY_TILE:(tt + 1) * KEY_TILE]
    qit_ref[...] = p[:, o_qi:o_u].T.astype(BF16)
    u_ref[...] = p[:, o_u:o_t]
    tail = p[:, o_t:o_t + TAIL]
    kx_ref[...] = tail[:, :IDX_HEAD_DIM].astype(BF16)
    wit_ref[...] = tail.T[IDX_HEAD_DIM:IDX_HEAD_DIM + N_IDX_HEADS] * (
        (N_IDX_HEADS ** -0.5) * (IDX_HEAD_DIM ** -0.5))


def _proj(x, g, w_cat, wuk, gkv, tm):
    B, S, D = x.shape
    ncol = w_cat.shape[1]
    grid = (B, S // tm)
    const = lambda *shape: pl.BlockSpec(shape, lambda b, i: (0,) * len(shape))
    return pl.pallas_call(
        _proj_kernel,
        grid=grid,
        in_specs=[
            pl.BlockSpec((None, tm, D), lambda b, i: (b, i, 0)),
            const(1, D),
            const(D, ncol),
            const(N_ATT_HEADS, ATT_HEAD_DIM, KV_LATENT),
            const(1, KV_LATENT),
        ],
        out_specs=[
            pl.BlockSpec((None, N_ATT_HEADS, KV_LATENT, tm), lambda b, i: (b, 0, 0, i)),
            pl.BlockSpec((None, tm, KV_LATENT), lambda b, i: (b, i, 0)),
            pl.BlockSpec((None, tm // KEY_TILE, KV_LATENT + ONES_ROWS, KEY_TILE), lambda b, i: (b, i, 0, 0)),
            pl.BlockSpec((None, COL_QI, tm), lambda b, i: (b, 0, i)),
            pl.BlockSpec((None, tm, IDX_HEAD_DIM), lambda b, i: (b, i, 0)),
            pl.BlockSpec((None, N_IDX_HEADS, tm), lambda b, i: (b, 0, i)),
            pl.BlockSpec((None, tm, POOL_WIDTH), lambda b, i: (b, i, 0)),
        ],
        out_shape=[
            jax.ShapeDtypeStruct((B, N_ATT_HEADS, KV_LATENT, S), BF16),
            jax.ShapeDtypeStruct((B, S, KV_LATENT), BF16),
            jax.ShapeDtypeStruct((B, S // KEY_TILE, KV_LATENT + ONES_ROWS, KEY_TILE), BF16),
            jax.ShapeDtypeStruct((B, COL_QI, S), BF16),
            jax.ShapeDtypeStruct((B, S, IDX_HEAD_DIM), BF16),
            jax.ShapeDtypeStruct((B, N_IDX_HEADS, S), F32),
            jax.ShapeDtypeStruct((B, S, POOL_WIDTH), F32),
        ],
        compiler_params=pltpu.CompilerParams(
            dimension_semantics=("parallel", "parallel"), vmem_limit_bytes=VMEM_LIMIT),
        name="proj",
    )(x, g, w_cat, wuk, gkv)


def _any(mask):
    return jnp.max(jnp.where(mask, 1.0, 0.0)) > 0.0


def _fold_rows(x, op, ways=FOLD_WAYS):
    groups = [x[g * SUBLANES:(g + 1) * SUBLANES] for g in range(x.shape[0] // SUBLANES)]
    parts = groups[:ways]
    for g, grp in enumerate(groups[ways:]):
        parts[g % ways] = op(parts[g % ways], grp)
    while len(parts) > 1:
        parts = [op(parts[i], parts[i + 1]) if i + 1 < len(parts) else parts[i] for i in range(0, len(parts), 2)]
    return parts[0]


def _attn_kernel(qlt_ref, qit_ref, wit_ref, kx_ref, c_ref, cxt_ref, wuvt_ref, a_ref,
                 sm_ref, dm_ref, lg_ref, mx_ref, ox_ref, *, n_top):
    qb = qit_ref.shape[1]
    n_tiles, _, tk = cxt_ref.shape
    dc = c_ref.shape[1]
    j = pl.program_id(1)
    nk = (j + 1) * (qb // tk)

    qpos = j * qb + lax.broadcasted_iota(jnp.int32, (1, qb), 1)
    qchunk = qpos // CHUNK
    n_adm = ((qchunk + 1) * CHUNK).astype(F32)
    k_sel = jnp.minimum(n_adm, float(n_top))
    pos_inf = jnp.full((1, qb), jnp.inf, F32)
    group_zeros = jnp.zeros((SUBLANES, qb), F32)
    group_inf = jnp.full((SUBLANES, qb), jnp.inf, F32)

    def kpos_of(t, n=tk):
        return t * n + lax.broadcasted_iota(jnp.int32, (n, 1), 0)

    n_pairs = (nk + 1) // 2

    def half_of(t):
        return pl.ds(pl.multiple_of((t % 2) * tk, tk), tk)

    @pl.when(nk % 2 == 1)
    def _():
        sm_ref[nk // 2, tk:, :] = jnp.full((tk, qb), -jnp.inf, F32)

    def keys_of(ref, t):
        return ref[pl.ds(pl.multiple_of(t * tk, tk), tk), :]

    def col_sum(x):
        return jnp.sum(x, axis=0, keepdims=True)

    def loop2(n, body, init):
        carry = lax.fori_loop(0, n // 2, lambda i, c: body(2 * i + 1, body(2 * i, c)), init)
        return lax.cond(n % 2 == 1, lambda c: body(n - 1, c), lambda c: c, carry)

    def loop4(n, body, init):
        def four(i, c):
            for u in range(4):
                c = body(4 * i + u, c)
            return c
        carry = lax.fori_loop(0, n // 4, four, init)
        base = (n // 4) * 4
        carry = lax.cond(n % 4 >= 2, lambda c: body(base + 1, body(base, c)), lambda c: c, carry)
        return lax.cond(n % 2 == 1, lambda c: body(n - 1, c), lambda c: c, carry)

    wit = wit_ref[...]

    def score_tile(t, carry, own_chunks):
        mn, mx = carry
        kx = keys_of(kx_ref, t)
        score = None
        for i in range(N_IDX_HEADS):
            rel = jnp.dot(kx, qit_ref[i * IDX_HEAD_DIM:(i + 1) * IDX_HEAD_DIM, :],
                          preferred_element_type=F32)
            term = wit[i:i + 1, :] * jnp.maximum(rel, 0.0)
            score = term if score is None else score + term
        if own_chunks:
            adm = (kpos_of(t) // CHUNK) <= qchunk
            lowest, score = jnp.where(adm, score, jnp.inf), jnp.where(adm, score, -jnp.inf)
        else:
            lowest = score
        sm_ref[t // 2, half_of(t), :] = score
        return (jnp.minimum(mn, _fold_rows(lowest, jnp.minimum)), jnp.maximum(mx, _fold_rows(score, jnp.maximum)))

    n_own = qb // tk
    carry = loop4(nk - n_own, functools.partial(score_tile, own_chunks=False), (group_inf, -group_inf))
    for own in range(n_own):
        carry = score_tile(nk - n_own + own, carry, own_chunks=True)
    mn, mx = carry
    q_min = jnp.min(mn, axis=0, keepdims=True)
    q_max = jnp.max(mx, axis=0, keepdims=True)

    state = (q_min, q_max + (jnp.abs(q_max) * (2.0 ** -20) + 1e-30), n_adm, jnp.zeros_like(n_adm))

    def count_ge(mid, pairs=None):
        def body(pr, acc):
            return acc + _fold_rows(jnp.where(sm_ref[pr] >= mid, 1.0, 0.0), jnp.add)
        if pairs is None:
            return col_sum(loop2(n_pairs, body, group_zeros))
        acc = group_zeros
        for pr in range(pairs):
            acc = body(pr, acc)
        return col_sum(acc)

    def bisect(n, st, pairs=None):
        lo, hi, clo, chi = st
        for _ in range(n):
            mid = lo + 0.5 * (hi - lo)
            cnt = count_ge(mid, pairs)
            ge = cnt >= k_sel
            lo, hi = jnp.where(ge, mid, lo), jnp.where(ge, hi, mid)
            clo, chi = jnp.where(ge, cnt, clo), jnp.where(ge, chi, cnt)
        return lo, hi, clo, chi

    def in_range_ends(st):
        lo, hi = st[0], st[1]

        def body(pr, carry):
            s = sm_ref[pr]
            return (jnp.minimum(carry[0], _fold_rows(jnp.where(s >= lo, s, jnp.inf), jnp.minimum)),
                    jnp.maximum(carry[1], _fold_rows(jnp.where(s < hi, s, -jnp.inf), jnp.maximum)))

        v_lo, v_hi = loop2(n_pairs, body, (group_inf, -group_inf))
        return jnp.min(v_lo, axis=0, keepdims=True), jnp.max(v_hi, axis=0, keepdims=True)

    def peel(st):
        lo, hi, clo, chi = st
        v_lo, v_hi = in_range_ends(st)
        done = clo == k_sel
        one_short = jnp.logical_and(jnp.logical_not(done), k_sel - chi == 1.0)
        one_over = jnp.logical_and(jnp.logical_not(jnp.logical_or(done, one_short)), clo - k_sel == 1.0)
        settled = jnp.logical_or(jnp.logical_or(done, one_short), jnp.logical_or(one_over, v_lo == v_hi))
        return (jnp.where(one_short, v_hi, lo), jnp.where(one_over, v_lo, pos_inf),
                _any(jnp.logical_not(settled)))

    def first_steps(st, pairs=1):
        run = functools.partial(bisect, FIRST_BISECT_STEPS, pairs=pairs)
        if pairs == (n_tiles + 1) // 2:
            return run(st)
        return lax.cond(n_pairs == pairs, run, functools.partial(first_steps, pairs=pairs + 1), st)

    state = lax.cond(nk * tk > n_top, first_steps, lambda st: st, state)

    def refine(carry):
        _, rounds, st, _, _ = carry
        lo_sel, drop, is_open = peel(st)
        st = lax.cond(is_open, functools.partial(bisect, MORE_BISECT_STEPS), lambda s: s, st)
        return is_open.astype(jnp.int32), rounds + 1, st, lo_sel, drop

    undecided = _any(state[2] != k_sel)
    _, _, state, lo_sel, drop = lax.while_loop(
        lambda c: jnp.logical_and(c[0] > 0, c[1] < MAX_BISECT_ROUNDS), refine,
        (undecided.astype(jnp.int32), jnp.int32(0), state, state[0], pos_inf))

    def select_pair(pr, acc):
        s = sm_ref[pr]
        sel = jnp.logical_and(s >= lo_sel, s != drop)
        dist = jnp.abs(qpos - kpos_of(pr, 2 * tk)).astype(F32)
        dm_ref[pr] = jnp.where(sel, dist, MASKED_DIST)
        return acc + _fold_rows(jnp.where(sel, 1.0, 0.0), jnp.add)

    n_sel = col_sum(loop2(n_pairs, select_pair, group_zeros))

    wrong = n_sel != k_sel

    @pl.when(_any(wrong))
    def _():
        def unresolved(st):
            v_lo, v_hi = in_range_ends(st)
            open_q = jnp.logical_and(wrong, jnp.logical_and(st[2] != k_sel, v_lo != v_hi))
            return _any(open_q).astype(jnp.int32)

        def tighten(carry):
            _, rounds, st = carry
            st = bisect(TIE_BISECT_STEPS, st)
            return unresolved(st), rounds + 1, st

        _, _, (lo, hi, _, chi) = lax.while_loop(
            lambda c: jnp.logical_and(c[0] > 0, c[1] < MAX_BISECT_ROUNDS), tighten,
            (unresolved(state), jnp.int32(0), state))
        need = k_sel - chi
        key_i = lax.broadcasted_iota(jnp.int32, (tk, tk), 0)
        tri = jnp.where(key_i >= lax.broadcasted_iota(jnp.int32, (tk, tk), 1), 1.0, 0.0).astype(BF16)

        def reselect_tile(t, seen):
            s = sm_ref[t // 2, half_of(t), :]
            in_range = jnp.logical_and(s >= lo, s < hi)
            ones = jnp.where(in_range, 1.0, 0.0)
            rank = seen + jnp.dot(tri, ones.astype(BF16), preferred_element_type=F32)
            sel_x = jnp.logical_or(s >= hi, jnp.logical_and(in_range, rank <= need))
            exact = jnp.where(sel_x, jnp.abs(qpos - kpos_of(t)).astype(F32), MASKED_DIST)
            dm_ref[t // 2, half_of(t), :] = jnp.where(wrong, exact, dm_ref[t // 2, half_of(t), :])
            return seen + col_sum(_fold_rows(ones, jnp.add))

        loop2(nk, reselect_tile, jnp.zeros((1, qb), F32))

    slopes = [LOG2E * 2.0 ** (-8.0 * (hd + 1) / N_ATT_HEADS) for hd in range(N_ATT_HEADS)]
    for hd in range(N_ATT_HEADS):
        mx_ref[hd] = -group_inf
        ox_ref[hd] = jnp.zeros(ox_ref.shape[1:], F32)

    def top_tile(t, _):
        dm = dm_ref[t // 2, half_of(t), :]
        c = keys_of(c_ref, t)
        for hd in range(N_ATT_HEADS):
            logits = jnp.dot(c, qlt_ref[hd], preferred_element_type=F32) - slopes[hd] * dm
            lg_ref[hd, t] = logits
            mx_ref[hd] = jnp.maximum(mx_ref[hd], _fold_rows(logits, jnp.maximum, ways=1))
        return 0

    loop4(nk, top_tile, 0)
    q_top = [jnp.max(mx_ref[hd], axis=0, keepdims=True) for hd in range(N_ATT_HEADS)]

    def pv_tile(t, _):
        cxt = cxt_ref[t]
        for hd in range(N_ATT_HEADS):
            p = jnp.exp2(lg_ref[hd, t] - q_top[hd]).astype(BF16)
            ox_ref[hd] += jnp.dot(cxt, p, preferred_element_type=F32)
        return 0

    loop4(nk, pv_tile, 0)
    a_heads = []
    for hd in range(N_ATT_HEADS):
        ox = ox_ref[hd]
        o_t = (ox[:dc] / ox[dc:dc + 1]).astype(BF16)
        a_heads.append(jnp.dot(wuvt_ref[hd], o_t, preferred_element_type=F32))
    a_ref[...] = jnp.concatenate(a_heads, axis=0).T.astype(BF16)


def _attn(qlt, qit, wit, kx, c, cxt, wuvt, qb, n_top):
    B, H, dc, S = qlt.shape
    n_tiles, ox_rows, tk = cxt.shape[1:]
    grid = (B, S // qb)
    return pl.pallas_call(
        functools.partial(_attn_kernel, n_top=n_top),
        grid=grid,
        in_specs=[
            pl.BlockSpec((None, H, dc, qb), lambda b, j: (b, 0, 0, j)),
            pl.BlockSpec((None, COL_QI, qb), lambda b, j: (b, 0, j)),
            pl.BlockSpec((None, N_IDX_HEADS, qb), lambda b, j: (b, 0, j)),
            pl.BlockSpec((None, S, IDX_HEAD_DIM), lambda b, j: (b, 0, 0)),
            pl.BlockSpec((None, S, dc), lambda b, j: (b, 0, 0)),
            pl.BlockSpec((None, n_tiles, ox_rows, tk), lambda b, j: (b, 0, 0, 0)),
            pl.BlockSpec((H, ATT_HEAD_DIM, dc), lambda b, j: (0, 0, 0)),
        ],
        out_specs=pl.BlockSpec((None, qb, COL_Q), lambda b, j: (b, j, 0)),
        out_shape=jax.ShapeDtypeStruct((B, S, COL_Q), BF16),
        scratch_shapes=[
            pltpu.VMEM(((n_tiles + 1) // 2, 2 * tk, qb), F32),
            pltpu.VMEM(((n_tiles + 1) // 2, 2 * tk, qb), F32),
            pltpu.VMEM((H, n_tiles, tk, qb), F32),
            pltpu.VMEM((H, SUBLANES, qb), F32),
            pltpu.VMEM((H, ox_rows, qb), F32),
        ],
        compiler_params=pltpu.CompilerParams(
            dimension_semantics=("parallel", "parallel"), vmem_limit_bytes=VMEM_LIMIT),
        name="attn",
    )(qlt, qit, wit, kx, c, cxt, wuvt)


def _mixffn_kernel(x_ref, xh_ref, a_ref, ah_ref, u_ref, uh_ref, wp_ref, ps_ref, wo_ref,
                   g_ref, wup_ref, cw_ref, cb_ref, wdn_ref, gf_ref, o_ref, act_ref, *, final_norm):
    tm, D = x_ref.shape
    n_chunks, _, fc2 = wup_ref.shape
    fc = fc2 // 2
    i = pl.program_id(1)
    rows = CONV_HALO + tm
    first = i == 0

    u_hist = jnp.where(first, 0.0, uh_ref[...])
    ext = jnp.concatenate([u_hist, u_ref[...]], axis=0)
    t = i * tm - CONV_HALO + lax.broadcasted_iota(jnp.int32, (rows, 1), 0)
    mixed = []
    for g, win in enumerate(POOL_WINDOWS):
        e = ext[:, g * POOL_GROUP:(g + 1) * POOL_GROUP]
        acc = e
        span = 1
        while span < win:
            acc = acc + pltpu.roll(acc, span, axis=0)
            span *= 2
        count = jnp.clip(t + 1, 1, win).astype(F32)
        pooled = acc[U_HALO - CONV_HALO:] / count - e[U_HALO - CONV_HALO:]
        mixed.append(jnp.dot(pooled.astype(BF16), wp_ref[g], preferred_element_type=F32))
    b = (jnp.concatenate(mixed, axis=1) * ps_ref[...]).astype(BF16)
    a = jnp.concatenate([ah_ref[A_HALO - CONV_HALO:], a_ref[...]], axis=0)
    xe = jnp.concatenate([xh_ref[...], x_ref[...]], axis=0)
    xe = xe + jnp.dot(jnp.concatenate([a, b], axis=1), wo_ref[...], preferred_element_type=F32)
    x = xe[CONV_HALO:]

    h = _rms(xe, g_ref[...])
    row = lax.broadcasted_iota(jnp.int32, (rows, 1), 0)
    h = jnp.where(jnp.logical_and(first, row < CONV_HALO), 0.0, h).astype(BF16)

    for k in range(n_chunks):
        up = jnp.dot(h, wup_ref[k], preferred_element_type=F32)
        cw = cw_ref[k]
        conv = cb_ref[k] + cw[2:3] * up
        for jj in range(CONV_WIDTH - 1):
            conv = conv + cw[jj:jj + 1] * pltpu.roll(up, CONV_WIDTH - 1 - jj, axis=0)
        conv = conv[CONV_HALO:]
        gate = conv[:, :fc]
        act = gate * (1.0 / (1.0 + jnp.exp(-gate))) * conv[:, fc:]
        act_ref[:, k * fc:(k + 1) * fc] = act.astype(BF16)

    y = x + jnp.dot(act_ref[...], wdn_ref[...], preferred_element_type=F32)
    if final_norm:
        y = _rms(y, gf_ref[...])
    o_ref[...] = y


def _mixffn(x, a, u, w_pool, pool_scale, w_o, g, wup, cw, cb, wdn, g_final, tm, final_norm):
    B, S, D = x.shape
    grid = (B, S // tm)
    hist = lambda rows: (lambda b, i: (b, jnp.maximum(i * (tm // rows) - 1, 0), 0))
    tile = lambda b, i: (b, i, 0)
    const = lambda *shape: pl.BlockSpec(shape, lambda b, i: (0,) * len(shape), pipeline_mode=pl.Buffered(1))
    return pl.pallas_call(
        functools.partial(_mixffn_kernel, final_norm=final_norm),
        grid=grid,
        in_specs=[
            pl.BlockSpec((None, tm, D), tile),
            pl.BlockSpec((None, CONV_HALO, D), hist(CONV_HALO)),
            pl.BlockSpec((None, tm, COL_Q), tile),
            pl.BlockSpec((None, A_HALO, COL_Q), hist(A_HALO)),
            pl.BlockSpec((None, tm, POOL_WIDTH), tile),
            pl.BlockSpec((None, U_HALO, POOL_WIDTH), hist(U_HALO)),
            const(*w_pool.shape),
            const(1, POOL_WIDTH),
            const(*w_o.shape),
            const(1, D),
            const(*wup.shape),
            const(*cw.shape),
            const(*cb.shape),
            const(*wdn.shape),
            const(1, D),
        ],
        out_specs=pl.BlockSpec((None, tm, D), tile),
        out_shape=jax.ShapeDtypeStruct((B, S, D), F32),
        scratch_shapes=[pltpu.VMEM((tm, wdn.shape[0]), BF16)],
        compiler_params=pltpu.CompilerParams(
            dimension_semantics=("parallel", "parallel"), vmem_limit_bytes=VMEM_LIMIT),
        name="mixffn",
    )(x, x, a, a, u, u, w_pool, pool_scale, w_o, g, wup, cw, cb, wdn, g_final)


def _ffn_chunk(d_ff):
    for fc in (256, 128):
        if d_ff % fc == 0:
            return fc
    raise ValueError(f"unsupported FFN width {d_ff}")


def kernel(x, g_mix, w_in, g_kv, w_uk, w_uv, w_pool, pool_scale, w_o, g_ffn, w_up, conv_w, conv_b, w_down, g_final):
    B, S, D = x.shape
    depth = g_mix.shape[0]
    d_ff = w_down.shape[1]
    fc = _ffn_chunk(d_ff)
    n_chunks = d_ff // fc
    n_top = min(TOPK_MAX, S // 4)
    tm = min(512, S)
    tp = min(1024, S)
    qb = min(256, S)
    assert S % tm == 0 and S % tp == 0 and S % qb == 0 and tm % U_HALO == 0
    assert qb % KEY_TILE == 0 and tp % KEY_TILE == 0 and KEY_TILE % CHUNK == 0
    assert w_in.shape[2] == COL_Q + KV_LATENT + COL_QI + IDX_HEAD_DIM + N_IDX_HEADS + POOL_WIDTH

    s_q = COL_Q
    s_c = s_q + KV_LATENT
    s_qi = s_c + COL_QI
    s_ki = s_qi + IDX_HEAD_DIM
    s_wi = s_ki + N_IDX_HEADS
    row = lambda v: v.reshape(1, -1).astype(F32)

    for l in range(depth):
        w = w_in[l]
        w_cat = jnp.concatenate(
            [w[:, :s_q], w[:, s_q:s_c], w[:, s_c:s_qi], w[:, s_wi:], w[:, s_qi:s_ki], w[:, s_ki:s_wi],
             jnp.zeros((D, TAIL - IDX_HEAD_DIM - N_IDX_HEADS), w.dtype)], axis=1).astype(BF16)
        wuvt = jnp.swapaxes(w_uv[l], 1, 2).astype(BF16)
        pair = lambda m: jnp.concatenate(
            [m[..., :d_ff].reshape(m.shape[:-1] + (n_chunks, fc)),
             m[..., d_ff:].reshape(m.shape[:-1] + (n_chunks, fc))], axis=-1)
        wup = jnp.moveaxis(pair(w_up[l]), 1, 0).astype(BF16)
        cw = jnp.moveaxis(pair(conv_w[l]), 1, 0).astype(F32)
        cb = pair(conv_b[l]).reshape(n_chunks, 1, 2 * fc).astype(F32)
        wdn = w_down[l].astype(BF16)

        qlt, c, cxt, qit, kx, wit, u = _proj(x, row(g_mix[l]), w_cat, w_uk[l].astype(BF16), row(g_kv[l]), tp)
        a = _attn(qlt, qit, wit, kx, c, cxt, wuvt, qb, n_top)
        x = _mixffn(x, a, u, w_pool[l].astype(BF16), row(pool_scale[l]), w_o[l].astype(BF16),
                    row(g_ffn[l]), wup, cw, cb, wdn, row(g_final), tm, final_norm=(l == depth - 1))
    return x
```

```python
import functools

import jax
import jax.numpy as jnp
from jax import lax
from jax.experimental import pallas as pl
from jax.experimental.pallas import tpu as pltpu

CHUNK = 64
N_ATT_HEADS = 8
ATT_HEAD_DIM = 64
KV_LATENT = 128
N_IDX_HEADS = 8
IDX_HEAD_DIM = 64
TOPK_MAX = 256
POOL_WINDOWS = (2, 4, 8, 16)
POOL_GROUP = 128
CONV_WIDTH = 3
EPS = 1e-6

COL_Q = N_ATT_HEADS * ATT_HEAD_DIM
COL_QI = N_IDX_HEADS * IDX_HEAD_DIM
POOL_WIDTH = POOL_GROUP * len(POOL_WINDOWS)
TAIL = 128
U_HALO = 32
A_HALO = 16
CONV_HALO = 8
MASKED_DIST = 1e30
FIRST_BISECT_STEPS = 14
MORE_BISECT_STEPS = 2
TIE_BISECT_STEPS = 8
MAX_BISECT_ROUNDS = 100
ONES_ROWS = 16
SUBLANES = 8
FOLD_WAYS = 4
KEY_TILE = 256
LOG2E = 1.4426950408889634
VMEM_LIMIT = 56 * 1024 * 1024

F32 = jnp.float32
BF16 = jnp.bfloat16


def _rms(x, g):
    return x * lax.rsqrt(jnp.mean(x * x, axis=-1, keepdims=True) + EPS) * g


def _proj_kernel(x_ref, g_ref, w_ref, wuk_ref, gkv_ref,
                 qlt_ref, c_ref, cxt_ref, qit_ref, kx_ref, wit_ref, u_ref):
    h = _rms(x_ref[...], g_ref[...]).astype(BF16)
    p = jnp.dot(h, w_ref[...], preferred_element_type=F32)
    tm = p.shape[0]
    o_c = COL_Q
    o_qi = o_c + KV_LATENT
    o_u = o_qi + COL_QI
    o_t = o_u + POOL_WIDTH
    q = p[:, :o_c].astype(BF16)
    for hd in range(N_ATT_HEADS):
        ql = jnp.dot(q[:, hd * ATT_HEAD_DIM:(hd + 1) * ATT_HEAD_DIM], wuk_ref[hd],
                     preferred_element_type=F32)
        qlt_ref[hd] = (ql * (ATT_HEAD_DIM ** -0.5 * LOG2E)).T.astype(BF16)
    c = _rms(p[:, o_c:o_qi], gkv_ref[...])
    c_ref[...] = c.astype(BF16)
    ones_rows = jnp.where(lax.broadcasted_iota(jnp.int32, (ONES_ROWS, tm), 0) == 0, 1.0, 0.0)
    cxt = jnp.concatenate([c.T, ones_rows], axis=0).astype(BF16)
    for tt in range(cxt_ref.shape[0]):
        cxt_ref[tt] = cxt[:, tt * KEY_TILE:(tt + 1) * KEY_TILE]
    qit_ref[...] = p[:, o_qi:o_u].T.astype(BF16)
    u_ref[...] = p[:, o_u:o_t]
    tail = p[:, o_t:o_t + TAIL]
    kx_ref[...] = tail[:, :IDX_HEAD_DIM].astype(BF16)
    wit_ref[...] = tail.T[IDX_HEAD_DIM:IDX_HEAD_DIM + N_IDX_HEADS] * (
        (N_IDX_HEADS ** -0.5) * (IDX_HEAD_DIM ** -0.5))


def _proj(x, g, w_cat, wuk, gkv, tm):
    B, S, D = x.shape
    ncol = w_cat.shape[1]
    grid = (B, S // tm)
    const = lambda *shape: pl.BlockSpec(shape, lambda b, i: (0,) * len(shape))
    return pl.pallas_call(
        _proj_kernel,
        grid=grid,
        in_specs=[
            pl.BlockSpec((None, tm, D), lambda b, i: (b, i, 0)),
            const(1, D),
            const(D, ncol),
            const(N_ATT_HEADS, ATT_HEAD_DIM, KV_LATENT),
            const(1, KV_LATENT),
        ],
        out_specs=[
            pl.BlockSpec((None, N_ATT_HEADS, KV_LATENT, tm), lambda b, i: (b, 0, 0, i)),
            pl.BlockSpec((None, tm, KV_LATENT), lambda b, i: (b, i, 0)),
            pl.BlockSpec((None, tm // KEY_TILE, KV_LATENT + ONES_ROWS, KEY_TILE), lambda b, i: (b, i, 0, 0)),
            pl.BlockSpec((None, COL_QI, tm), lambda b, i: (b, 0, i)),
            pl.BlockSpec((None, tm, IDX_HEAD_DIM), lambda b, i: (b, i, 0)),
            pl.BlockSpec((None, N_IDX_HEADS, tm), lambda b, i: (b, 0, i)),
            pl.BlockSpec((None, tm, POOL_WIDTH), lambda b, i: (b, i, 0)),
        ],
        out_shape=[
            jax.ShapeDtypeStruct((B, N_ATT_HEADS, KV_LATENT, S), BF16),
            jax.ShapeDtypeStruct((B, S, KV_LATENT), BF16),
            jax.ShapeDtypeStruct((B, S // KEY_TILE, KV_LATENT + ONES_ROWS, KEY_TILE), BF16),
            jax.ShapeDtypeStruct((B, COL_QI, S), BF16),
            jax.ShapeDtypeStruct((B, S, IDX_HEAD_DIM), BF16),
            jax.ShapeDtypeStruct((B, N_IDX_HEADS, S), F32),
            jax.ShapeDtypeStruct((B, S, POOL_WIDTH), F32),
        ],
        compiler_params=pltpu.CompilerParams(
            dimension_semantics=("parallel", "parallel"), vmem_limit_bytes=VMEM_LIMIT),
        name="proj",
    )(x, g, w_cat, wuk, gkv)


def _any(mask):
    return jnp.max(jnp.where(mask, 1.0, 0.0)) > 0.0


def _fold_rows(x, op, ways=FOLD_WAYS):
    groups = [x[g * SUBLANES:(g + 1) * SUBLANES] for g in range(x.shape[0] // SUBLANES)]
    parts = groups[:ways]
    for g, grp in enumerate(groups[ways:]):
        parts[g % ways] = op(parts[g % ways], grp)
    while len(parts) > 1:
        parts = [op(parts[i], parts[i + 1]) if i + 1 < len(parts) else parts[i] for i in range(0, len(parts), 2)]
    return parts[0]


def _attn_kernel(qlt_ref, qit_ref, wit_ref, kx_ref, c_ref, cxt_ref, wuvt_ref, a_ref,
                 sm_ref, dm_ref, lg_ref, mx_ref, ox_ref, *, n_top):
    qb = qit_ref.shape[1]
    n_tiles, _, tk = cxt_ref.shape
    dc = c_ref.shape[1]
    j = pl.program_id(1)
    nk = (j + 1) * (qb // tk)

    qpos = j * qb + lax.broadcasted_iota(jnp.int32, (1, qb), 1)
    qchunk = qpos // CHUNK
    n_adm = ((qchunk + 1) * CHUNK).astype(F32)
    k_sel = jnp.minimum(n_adm, float(n_top))
    pos_inf = jnp.full((1, qb), jnp.inf, F32)
    group_zeros = jnp.zeros((SUBLANES, qb), F32)
    group_inf = jnp.full((SUBLANES, qb), jnp.inf, F32)

    def kpos_of(t, n=tk):
        return t * n + lax.broadcasted_iota(jnp.int32, (n, 1), 0)

    n_pairs = (nk + 1) // 2

    def half_of(t):
        return pl.ds(pl.multiple_of((t % 2) * tk, tk), tk)

    @pl.when(nk % 2 == 1)
    def _():
        sm_ref[nk // 2, tk:, :] = jnp.full((tk, qb), -jnp.inf, F32)

    def keys_of(ref, t):
        return ref[pl.ds(pl.multiple_of(t * tk, tk), tk), :]

    def col_sum(x):
        return jnp.sum(x, axis=0, keepdims=True)

    def loop2(n, body, init):
        carry = lax.fori_loop(0, n // 2, lambda i, c: body(2 * i + 1, body(2 * i, c)), init)
        return lax.cond(n % 2 == 1, lambda c: body(n - 1, c), lambda c: c, carry)

    def loop4(n, body, init):
        def four(i, c):
            for u in range(4):
                c = body(4 * i + u, c)
            return c
        carry = lax.fori_loop(0, n // 4, four, init)
        base = (n // 4) * 4
        carry = lax.cond(n % 4 >= 2, lambda c: body(base + 1, body(base, c)), lambda c: c, carry)
        return lax.cond(n % 2 == 1, lambda c: body(n - 1, c), lambda c: c, carry)

    wit = wit_ref[...]

    def score_tile(t, carry, own_chunks):
        mn, mx = carry
        kx = keys_of(kx_ref, t)
        score = None
        for i in range(N_IDX_HEADS):
            rel = jnp.dot(kx, qit_ref[i * IDX_HEAD_DIM:(i + 1) * IDX_HEAD_DIM, :],
                          preferred_element_type=F32)
            term = wit[i:i + 1, :] * jnp.maximum(rel, 0.0)
            score = term if score is None else score + term
        if own_chunks:
            adm = (kpos_of(t) // CHUNK) <= qchunk
            lowest, score = jnp.where(adm, score, jnp.inf), jnp.where(adm, score, -jnp.inf)
        else:
            lowest = score
        sm_ref[t // 2, half_of(t), :] = score
        return (jnp.minimum(mn, _fold_rows(lowest, jnp.minimum)), jnp.maximum(mx, _fold_rows(score, jnp.maximum)))

    n_own = qb // tk
    carry = loop4(nk - n_own, functools.partial(score_tile, own_chunks=False), (group_inf, -group_inf))
    for own in range(n_own):
        carry = score_tile(nk - n_own + own, carry, own_chunks=True)
    mn, mx = carry
    q_min = jnp.min(mn, axis=0, keepdims=True)
    q_max = jnp.max(mx, axis=0, keepdims=True)

    state = (q_min, q_max + (jnp.abs(q_max) * (2.0 ** -20) + 1e-30), n_adm, jnp.zeros_like(n_adm))

    def count_ge(mid, pairs=None):
        def body(pr, acc):
            return acc + _fold_rows(jnp.where(sm_ref[pr] >= mid, 1.0, 0.0), jnp.add)
        if pairs is None:
            return col_sum(loop2(n_pairs, body, group_zeros))
        acc = group_zeros
        for pr in range(pairs):
            acc = body(pr, acc)
        return col_sum(acc)

    def bisect(n, st, pairs=None):
        lo, hi, clo, chi = st
        for _ in range(n):
            mid = lo + 0.5 * (hi - lo)
            cnt = count_ge(mid, pairs)
            ge = cnt >= k_sel
            lo, hi = jnp.where(ge, mid, lo), jnp.where(ge, hi, mid)
            clo, chi = jnp.where(ge, cnt, clo), jnp.where(ge, chi, cnt)
        return lo, hi, clo, chi

    def in_range_ends(st):
        lo, hi = st[0], st[1]

        def body(pr, carry):
            s = sm_ref[pr]
            return (jnp.minimum(carry[0], _fold_rows(jnp.where(s >= lo, s, jnp.inf), jnp.minimum)),
                    jnp.maximum(carry[1], _fold_rows(jnp.where(s < hi, s, -jnp.inf), jnp.maximum)))

        v_lo, v_hi = loop2(n_pairs, body, (group_inf, -group_inf))
        return jnp.min(v_lo, axis=0, keepdims=True), jnp.max(v_hi, axis=0, keepdims=True)

    def peel(st):
        lo, hi, clo, chi = st
        v_lo, v_hi = in_range_ends(st)
        done = clo == k_sel
        one_short = jnp.logical_and(jnp.logical_not(done), k_sel - chi == 1.0)
        one_over = jnp.logical_and(jnp.logical_not(jnp.logical_or(done, one_short)), clo - k_sel == 1.0)
        settled = jnp.logical_or(jnp.logical_or(done, one_short), jnp.logical_or(one_over, v_lo == v_hi))
        return (jnp.where(one_short, v_hi, lo), jnp.where(one_over, v_lo, pos_inf),
                _any(jnp.logical_not(settled)))

    def first_steps(st, pairs=1):
        run = functools.partial(bisect, FIRST_BISECT_STEPS, pairs=pairs)
        if pairs == (n_tiles + 1) // 2:
            return run(st)
        return lax.cond(n_pairs == pairs, run, functools.partial(first_steps, pairs=pairs + 1), st)

    state = lax.cond(nk * tk > n_top, first_steps, lambda st: st, state)

    def refine(carry):
        _, rounds, st, _, _ = carry
        lo_sel, drop, is_open = peel(st)
        st = lax.cond(is_open, functools.partial(bisect, MORE_BISECT_STEPS), lambda s: s, st)
        return is_open.astype(jnp.int32), rounds + 1, st, lo_sel, drop

    undecided = _any(state[2] != k_sel)
    _, _, state, lo_sel, drop = lax.while_loop(
        lambda c: jnp.logical_and(c[0] > 0, c[1] < MAX_BISECT_ROUNDS), refine,
        (undecided.astype(jnp.int32), jnp.int32(0), state, state[0], pos_inf))

    def select_pair(pr, acc):
        s = sm_ref[pr]
        sel = jnp.logical_and(s >= lo_sel, s != drop)
        dist = jnp.abs(qpos - kpos_of(pr, 2 * tk)).astype(F32)
        dm_ref[pr] = jnp.where(sel, dist, MASKED_DIST)
        return acc + _fold_rows(jnp.where(sel, 1.0, 0.0), jnp.add)

    n_sel = col_sum(loop2(n_pairs, select_pair, group_zeros))

    wrong = n_sel != k_sel

    @pl.when(_any(wrong))
    def _():
        def unresolved(st):
            v_lo, v_hi = in_range_ends(st)
            open_q = jnp.logical_and(wrong, jnp.logical_and(st[2] != k_sel, v_lo != v_hi))
            return _any(open_q).astype(jnp.int32)

        def tighten(carry):
            _, rounds, st = carry
            st = bisect(TIE_BISECT_STEPS, st)
            return unresolved(st), rounds + 1, st

        _, _, (lo, hi, _, chi) = lax.while_loop(
            lambda c: jnp.logical_and(c[0] > 0, c[1] < MAX_BISECT_ROUNDS), tighten,
            (unresolved(state), jnp.int32(0), state))
        need = k_sel - chi
        key_i = lax.broadcasted_iota(jnp.int32, (tk, tk), 0)
        tri = jnp.where(key_i >= lax.broadcasted_iota(jnp.int32, (tk, tk), 1), 1.0, 0.0).astype(BF16)

        def reselect_tile(t, seen):
            s = sm_ref[t // 2, half_of(t), :]
            in_range = jnp.logical_and(s >= lo, s < hi)
            ones = jnp.where(in_range, 1.0, 0.0)
            rank = seen + jnp.dot(tri, ones.astype(BF16), preferred_element_type=F32)
            sel_x = jnp.logical_or(s >= hi, jnp.logical_and(in_range, rank <= need))
            exact = jnp.where(sel_x, jnp.abs(qpos - kpos_of(t)).astype(F32), MASKED_DIST)
            dm_ref[t // 2, half_of(t), :] = jnp.where(wrong, exact, dm_ref[t // 2, half_of(t), :])
            return seen + col_sum(_fold_rows(ones, jnp.add))

        loop2(nk, reselect_tile, jnp.zeros((1, qb), F32))

    slopes = [LOG2E * 2.0 ** (-8.0 * (hd + 1) / N_ATT_HEADS) for hd in range(N_ATT_HEADS)]
    for hd in range(N_ATT_HEADS):
        mx_ref[hd] = -group_inf
        ox_ref[hd] = jnp.zeros(ox_ref.shape[1:], F32)

    def top_tile(t, _):
        dm = dm_ref[t // 2, half_of(t), :]
        c = keys_of(c_ref, t)
        for hd in range(N_ATT_HEADS):
            logits = jnp.dot(c, qlt_ref[hd], preferred_element_type=F32) - slopes[hd] * dm
            lg_ref[hd, t] = logits
            mx_ref[hd] = jnp.maximum(mx_ref[hd], _fold_rows(logits, jnp.maximum, ways=1))
        return 0

    loop4(nk, top_tile, 0)
    q_top = [jnp.max(mx_ref[hd], axis=0, keepdims=True) for hd in range(N_ATT_HEADS)]

    def pv_tile(t, _):
        cxt = cxt_ref[t]
        for hd in range(N_ATT_HEADS):
            p = jnp.exp2(lg_ref[hd, t] - q_top[hd]).astype(BF16)
            ox_ref[hd] += jnp.dot(cxt, p, preferred_element_type=F32)
        return 0

    loop4(nk, pv_tile, 0)
    a_heads = []
    for hd in range(N_ATT_HEADS):
        ox = ox_ref[hd]
        o_t = (ox[:dc] / ox[dc:dc + 1]).astype(BF16)
        a_heads.append(jnp.dot(wuvt_ref[hd], o_t, preferred_element_type=F32))
    a_ref[...] = jnp.concatenate(a_heads, axis=0).T.astype(BF16)


def _attn(qlt, qit, wit, kx, c, cxt, wuvt, qb, n_top):
    B, H, dc, S = qlt.shape
    n_tiles, ox_rows, tk = cxt.shape[1:]
    grid = (B, S // qb)
    return pl.pallas_call(
        functools.partial(_attn_kernel, n_top=n_top),
        grid=grid,
        in_specs=[
            pl.BlockSpec((None, H, dc, qb), lambda b, j: (b, 0, 0, j)),
            pl.BlockSpec((None, COL_QI, qb), lambda b, j: (b, 0, j)),
            pl.BlockSpec((None, N_IDX_HEADS, qb), lambda b, j: (b, 0, j)),
            pl.BlockSpec((None, S, IDX_HEAD_DIM), lambda b, j: (b, 0, 0)),
            pl.BlockSpec((None, S, dc), lambda b, j: (b, 0, 0)),
            pl.BlockSpec((None, n_tiles, ox_rows, tk), lambda b, j: (b, 0, 0, 0)),
            pl.BlockSpec((H, ATT_HEAD_DIM, dc), lambda b, j: (0, 0, 0)),
        ],
        out_specs=pl.BlockSpec((None, qb, COL_Q), lambda b, j: (b, j, 0)),
        out_shape=jax.ShapeDtypeStruct((B, S, COL_Q), BF16),
        scratch_shapes=[
            pltpu.VMEM(((n_tiles + 1) // 2, 2 * tk, qb), F32),
            pltpu.VMEM(((n_tiles + 1) // 2, 2 * tk, qb), F32),
            pltpu.VMEM((H, n_tiles, tk, qb), F32),
            pltpu.VMEM((H, SUBLANES, qb), F32),
            pltpu.VMEM((H, ox_rows, qb), F32),
        ],
        compiler_params=pltpu.CompilerParams(
            dimension_semantics=("parallel", "parallel"), vmem_limit_bytes=VMEM_LIMIT),
        name="attn",
    )(qlt, qit, wit, kx, c, cxt, wuvt)


def _mixffn_kernel(x_ref, xh_ref, a_ref, ah_ref, u_ref, uh_ref, wp_ref, ps_ref, wo_ref,
                   g_ref, wup_ref, cw_ref, cb_ref, wdn_ref, gf_ref, o_ref, act_ref, *, final_norm):
    tm, D = x_ref.shape
    n_chunks, _, fc2 = wup_ref.shape
    fc = fc2 // 2
    i = pl.program_id(1)
    rows = CONV_HALO + tm
    first = i == 0

    u_hist = jnp.where(first, 0.0, uh_ref[...])
    ext = jnp.concatenate([u_hist, u_ref[...]], axis=0)
    t = i * tm - CONV_HALO + lax.broadcasted_iota(jnp.int32, (rows, 1), 0)
    mixed = []
    for g, win in enumerate(POOL_WINDOWS):
        e = ext[:, g * POOL_GROUP:(g + 1) * POOL_GROUP]
        acc = e
        span = 1
        while span < win:
            acc = acc + pltpu.roll(acc, span, axis=0)
            span *= 2
        count = jnp.clip(t + 1, 1, win).astype(F32)
        pooled = acc[U_HALO - CONV_HALO:] / count - e[U_HALO - CONV_HALO:]
        mixed.append(jnp.dot(pooled.astype(BF16), wp_ref[g], preferred_element_type=F32))
    b = (jnp.concatenate(mixed, axis=1) * ps_ref[...]).astype(BF16)
    a = jnp.concatenate([ah_ref[A_HALO - CONV_HALO:], a_ref[...]], axis=0)
    xe = jnp.concatenate([xh_ref[...], x_ref[...]], axis=0)
    xe = xe + jnp.dot(jnp.concatenate([a, b], axis=1), wo_ref[...], preferred_element_type=F32)
    x = xe[CONV_HALO:]

    h = _rms(xe, g_ref[...])
    row = lax.broadcasted_iota(jnp.int32, (rows, 1), 0)
    h = jnp.where(jnp.logical_and(first, row < CONV_HALO), 0.0, h).astype(BF16)

    for k in range(n_chunks):
        up = jnp.dot(h, wup_ref[k], preferred_element_type=F32)
        cw = cw_ref[k]
        conv = cb_ref[k] + cw[2:3] * up
        for jj in range(CONV_WIDTH - 1):
            conv = conv + cw[jj:jj + 1] * pltpu.roll(up, CONV_WIDTH - 1 - jj, axis=0)
        conv = conv[CONV_HALO:]
        gate = conv[:, :fc]
        act = gate * (1.0 + jnp.tanh(gate)) * conv[:, fc:]
        act_ref[:, k * fc:(k + 1) * fc] = act.astype(BF16)

    y = x + jnp.dot(act_ref[...], wdn_ref[...], preferred_element_type=F32)
    if final_norm:
        y = _rms(y, gf_ref[...])
    o_ref[...] = y


def _mixffn(x, a, u, w_pool, pool_scale, w_o, g, wup, cw, cb, wdn, g_final, tm, final_norm):
    B, S, D = x.shape
    grid = (B, S // tm)
    hist = lambda rows: (lambda b, i: (b, jnp.maximum(i * (tm // rows) - 1, 0), 0))
    tile = lambda b, i: (b, i, 0)
    const = lambda *shape: pl.BlockSpec(shape, lambda b, i: (0,) * len(shape), pipeline_mode=pl.Buffered(1))
    return pl.pallas_call(
        functools.partial(_mixffn_kernel, final_norm=final_norm),
        grid=grid,
        in_specs=[
            pl.BlockSpec((None, tm, D), tile),
            pl.BlockSpec((None, CONV_HALO, D), hist(CONV_HALO)),
            pl.BlockSpec((None, tm, COL_Q), tile),
            pl.BlockSpec((None, A_HALO, COL_Q), hist(A_HALO)),
            pl.BlockSpec((None, tm, POOL_WIDTH), tile),
            pl.BlockSpec((None, U_HALO, POOL_WIDTH), hist(U_HALO)),
            const(*w_pool.shape),
            const(1, POOL_WIDTH),
            const(*w_o.shape),
            const(1, D),
            const(*wup.shape),
            const(*cw.shape),
            const(*cb.shape),
            const(*wdn.shape),
            const(1, D),
        ],
        out_specs=pl.BlockSpec((None, tm, D), tile),
        out_shape=jax.ShapeDtypeStruct((B, S, D), F32),
        scratch_shapes=[pltpu.VMEM((tm, wdn.shape[0]), BF16)],
        compiler_params=pltpu.CompilerParams(
            dimension_semantics=("parallel", "parallel"), vmem_limit_bytes=VMEM_LIMIT),
        name="mixffn",
    )(x, x, a, a, u, u, w_pool, pool_scale, w_o, g, wup, cw, cb, wdn, g_final)


def _ffn_chunk(d_ff):
    for fc in (256, 128):
        if d_ff % fc == 0:
            return fc
    raise ValueError(f"unsupported FFN width {d_ff}")


def kernel(x, g_mix, w_in, g_kv, w_uk, w_uv, w_pool, pool_scale, w_o, g_ffn, w_up, conv_w, conv_b, w_down, g_final):
    B, S, D = x.shape
    depth = g_mix.shape[0]
    d_ff = w_down.shape[1]
    fc = _ffn_chunk(d_ff)
    n_chunks = d_ff // fc
    n_top = min(TOPK_MAX, S // 4)
    tm = min(512, S)
    tp = min(1024, S)
    qb = min(256, S)
    assert S % tm == 0 and S % tp == 0 and S % qb == 0 and tm % U_HALO == 0
    assert qb % KEY_TILE == 0 and tp % KEY_TILE == 0 and KEY_TILE % CHUNK == 0
    assert w_in.shape[2] == COL_Q + KV_LATENT + COL_QI + IDX_HEAD_DIM + N_IDX_HEADS + POOL_WIDTH

    s_q = COL_Q
    s_c = s_q + KV_LATENT
    s_qi = s_c + COL_QI
    s_ki = s_qi + IDX_HEAD_DIM
    s_wi = s_ki + N_IDX_HEADS
    row = lambda v: v.reshape(1, -1).astype(F32)

    for l in range(depth):
        w = w_in[l]
        w_cat = jnp.concatenate(
            [w[:, :s_q], w[:, s_q:s_c], w[:, s_c:s_qi], w[:, s_wi:], w[:, s_qi:s_ki], w[:, s_ki:s_wi],
             jnp.zeros((D, TAIL - IDX_HEAD_DIM - N_IDX_HEADS), w.dtype)], axis=1).astype(BF16)
        wuvt = jnp.swapaxes(w_uv[l], 1, 2).astype(BF16)
        pair = lambda m: jnp.concatenate(
            [m[..., :d_ff].reshape(m.shape[:-1] + (n_chunks, fc)),
             m[..., d_ff:].reshape(m.shape[:-1] + (n_chunks, fc))], axis=-1)
        wup = jnp.moveaxis(pair(w_up[l]), 1, 0).astype(BF16)
        halve_gate = jnp.concatenate([jnp.full((fc,), 0.5, F32), jnp.ones((fc,), F32)])
        cw = jnp.moveaxis(pair(conv_w[l]), 1, 0).astype(F32) * halve_gate
        cb = pair(conv_b[l]).reshape(n_chunks, 1, 2 * fc).astype(F32) * halve_gate
        wdn = w_down[l].astype(BF16)

        qlt, c, cxt, qit, kx, wit, u = _proj(x, row(g_mix[l]), w_cat, w_uk[l].astype(BF16), row(g_kv[l]), tp)
        a = _attn(qlt, qit, wit, kx, c, cxt, wuvt, qb, n_top)
        x = _mixffn(x, a, u, w_pool[l].astype(BF16), row(pool_scale[l]), w_o[l].astype(BF16),
                    row(g_ffn[l]), wup, cw, cb, wdn, row(g_final), tm, final_norm=(l == depth - 1))
    return x
```
